```python
import math
import jax, jax.numpy as jnp
from jax import lax
import numpy as np

D_MODEL = 1024
BATCH = 4
SEQ = 8192
DEPTH = 1
DEC_BATCH = 32
DEC_SEQ = 64
PAST_LEN = 1024

CHUNK = 64
WINDOW = 128
WIN_CHUNKS = WINDOW // CHUNK
HEAD_DIM = 64
N_Q_HEADS = 8
N_KV_HEADS = 2
Q_GROUP = N_Q_HEADS // N_KV_HEADS
ATTN_WIDTH = N_Q_HEADS * HEAD_DIM
KV_WIDTH = N_KV_HEADS * HEAD_DIM
HGRN_HEADS = 8
HGRN_DK = 128
HGRN_DV = 128
HGRN_WIDTH = HGRN_HEADS * HGRN_DK
HGRN_BLOCK = 16
N_BRANCH = 2
D_FF = -(-8 * D_MODEL // (3 * 256)) * 256
IN_WIDTH = ATTN_WIDTH + 2 * KV_WIDTH + 4 * HGRN_WIDTH + N_BRANCH * D_MODEL
ALPHA = (2 * DEPTH) ** 0.25
BETA = (8 * DEPTH) ** -0.25
LN_EPS = 1e-5
RMS_EPS = 1e-6

kernel_name = "swa_sink_hgrn2_griffin_merge_deepnorm_adaln"


def split_points():
    sizes = (ATTN_WIDTH, KV_WIDTH, KV_WIDTH, HGRN_WIDTH, HGRN_WIDTH, HGRN_WIDTH, HGRN_WIDTH)
    return [int(v) for v in np.cumsum(sizes)]


def layer_norm(x, gain=None, bias=None):
    xf = x.astype(jnp.float32)
    mu = jnp.mean(xf, axis=-1, keepdims=True)
    var = jnp.mean(jnp.square(xf - mu), axis=-1, keepdims=True)
    y = (xf - mu) * lax.rsqrt(var + LN_EPS)
    if gain is not None:
        y = y * gain.astype(jnp.float32) + bias.astype(jnp.float32)
    return y.astype(x.dtype)


def rms_norm(x):
    xf = x.astype(jnp.float32)
    return xf * lax.rsqrt(jnp.mean(jnp.square(xf), axis=-1, keepdims=True) + RMS_EPS)


def alibi_slopes():
    return jnp.exp2(-8.0 * jnp.arange(1, N_Q_HEADS + 1, dtype=jnp.float32) / N_Q_HEADS)


def banded_attention(q, k, v, qpos, kpos, valid, sinks):
    B, N, Tq = q.shape[:3]
    qg = q.reshape(B, N, Tq, N_KV_HEADS, Q_GROUP, HEAD_DIM)
    s = jnp.einsum('bnqhgd,bnkhd->bnhgqk', qg, k, preferred_element_type=jnp.float32) * (HEAD_DIM ** -0.5)
    dist = jnp.abs(qpos[:, :, None] - kpos[:, None, :]).astype(jnp.float32)
    s = s - alibi_slopes().reshape(N_KV_HEADS, Q_GROUP, 1, 1) * dist[None, :, None, None]
    s = jnp.where(valid[None, :, None, None], s, -jnp.inf)
    sink = jnp.broadcast_to(sinks.astype(jnp.float32).reshape(N_KV_HEADS, Q_GROUP, 1, 1), s.shape[:-1] + (1,))
    p = jax.nn.softmax(jnp.concatenate([s, sink], axis=-1), axis=-1)[..., :-1]
    o = jnp.einsum('bnhgqk,bnkhd->bnqhgd', p.astype(v.dtype), v)
    return o.reshape(B, N, Tq, ATTN_WIDTH)


def swa_prompt(q, k, v, sinks):
    B, T = q.shape[:2]
    N = T // CHUNK
    qc = q.reshape(B, N, CHUNK, N_Q_HEADS, HEAD_DIM)
    pad = ((0, 0), (WINDOW, 0), (0, 0), (0, 0))
    kp = jnp.pad(k, pad).reshape(B, N + WIN_CHUNKS, CHUNK, N_KV_HEADS, HEAD_DIM)
    vp = jnp.pad(v, pad).reshape(B, N + WIN_CHUNKS, CHUNK, N_KV_HEADS, HEAD_DIM)
    kb = jnp.concatenate([kp[:, j:j + N] for j in range(WIN_CHUNKS + 1)], axis=2)
    vb = jnp.concatenate([vp[:, j:j + N] for j in range(WIN_CHUNKS + 1)], axis=2)
    qpos = jnp.arange(T).reshape(N, CHUNK)
    kpos = (jnp.arange(N)[:, None] - WIN_CHUNKS) * CHUNK + jnp.arange((WIN_CHUNKS + 1) * CHUNK)[None, :]
    valid = (kpos >= 0)[:, None, :]
    o = banded_attention(qc, kb, vb, qpos, kpos, valid, sinks).reshape(B, T, ATTN_WIDTH)
    return o, k[:, T - WINDOW:], v[:, T - WINDOW:]


def swa_sample(q, k, v, k_cache, v_cache, sinks):
    T = q.shape[1]
    kb = jnp.concatenate([k_cache.astype(k.dtype), k], axis=1)
    vb = jnp.concatenate([v_cache.astype(v.dtype), v], axis=1)
    qpos = (PAST_LEN + jnp.arange(T))[None, :]
    kpos = (PAST_LEN - WINDOW + jnp.arange(WINDOW + T))[None, :]
    valid = jnp.ones((1, 1, WINDOW + T), dtype=bool)
    o = banded_attention(q[:, None], kb[:, None], vb[:, None], qpos, kpos, valid, sinks)[:, 0]
    return o, kb[:, T:], vb[:, T:]


def hgrn2_recurrence(q, log_f, k, v, s0):
    B, T = q.shape[:2]
    L = HGRN_BLOCK
    Tp = -(-T // L) * L
    pad = ((0, 0), (0, Tp - T), (0, 0), (0, 0))
    nb = Tp // L

    def blocks(a):
        return jnp.pad(a, pad).reshape(B, nb, L, HGRN_HEADS, a.shape[-1]).astype(jnp.float32)

    q, log_f, k, v = blocks(q), blocks(log_f), blocks(k), blocks(v)
    b = jnp.cumsum(log_f, axis=2)
    b_last = b[:, :, -1:]
    qe = q * jnp.exp(b)
    ke = k * jnp.exp(-b)
    kd = k * jnp.exp(b_last - b)
    causal = jnp.tril(jnp.ones((L, L), dtype=bool))
    a = jnp.where(causal, jnp.einsum('bnlhd,bnmhd->bnhlm', qe, ke), 0.0)
    o_intra = jnp.einsum('bnhlm,bnmhe->bnlhe', a, v)

    def step(s, inp):
        qe_n, kd_n, v_n, dec_n = inp
        o = jnp.einsum('blhd,bhde->blhe', qe_n, s)
        s = dec_n[..., None] * s + jnp.einsum('blhd,blhe->bhde', kd_n, v_n)
        return s, o

    xs = (jnp.moveaxis(qe, 1, 0), jnp.moveaxis(kd, 1, 0), jnp.moveaxis(v, 1, 0),
          jnp.moveaxis(jnp.exp(b_last[:, :, 0]), 1, 0))
    s_final, o_inter = lax.scan(step, s0.astype(jnp.float32), xs)
    o = o_intra + jnp.moveaxis(o_inter, 0, 1)
    return o.reshape(B, Tp, HGRN_HEADS, HGRN_DV)[:, :T], s_final.astype(s0.dtype)


def hgrn2_branch(hq, hf, hi, hg, lb, norm_w, s0):
    B, T = hq.shape[:2]
    shp = (B, T, HGRN_HEADS, HGRN_DK)
    z = hf.astype(jnp.float32).reshape(shp)
    lbh = lb.reshape(HGRN_HEADS, HGRN_DK)
    log_f = jnp.log(lbh + (1.0 - lbh) * jax.nn.sigmoid(z))
    k_in = (1.0 - lbh) * jax.nn.sigmoid(-z)
    q_r = jax.nn.silu(hq).reshape(shp)
    v_r = hi.reshape(B, T, HGRN_HEADS, HGRN_DV)
    o, s_new = hgrn2_recurrence(q_r, log_f, k_in, v_r, s0)
    o = rms_norm(o) * norm_w.astype(jnp.float32).reshape(HGRN_HEADS, HGRN_DV)
    y = o.reshape(B, T, HGRN_WIDTH).astype(hg.dtype) * jax.nn.silu(hg)
    return y, s_new


def layer_forward(x, c, k_cache, v_cache, s0, w_ada, b_ada, w_in, sinks, lb, hgrn_norm_w,
                  w_branch_attn, w_branch_hgrn, w_out, ln_mix_g, ln_mix_b,
                  w_up, w_down, ln_ffn_g, ln_ffn_b):
    B, T = x.shape[:2]
    mod = jnp.einsum('bd,de->be', jax.nn.silu(c), w_ada) + b_ada
    sh1, sc1, g1, sh2, sc2, g2 = jnp.split(mod[:, None, :], 6, axis=-1)
    h = layer_norm(x) * (1 + sc1) + sh1
    proj = jnp.einsum('btd,de->bte', h, w_in)
    q, k, v, hq, hf, hi, hg, gates = jnp.split(proj, split_points(), axis=-1)
    q = q.reshape(B, T, N_Q_HEADS, HEAD_DIM)
    k = k.reshape(B, T, N_KV_HEADS, HEAD_DIM)
    v = v.reshape(B, T, N_KV_HEADS, HEAD_DIM)
    if k_cache is None:
        ya, k_win, v_win = swa_prompt(q, k, v, sinks)
        s0 = jnp.zeros((B, HGRN_HEADS, HGRN_DK, HGRN_DV), dtype=x.dtype)
    else:
        ya, k_win, v_win = swa_sample(q, k, v, k_cache, v_cache, sinks)
    yh, s_new = hgrn2_branch(hq, hf, hi, hg, lb, hgrn_norm_w, s0)
    ga, gh = jnp.split(gates, 2, axis=-1)
    merged = jax.nn.sigmoid(ga) * (ya @ w_branch_attn) + jax.nn.sigmoid(gh) * (yh @ w_branch_hgrn)
    x = layer_norm(ALPHA * x + g1 * (merged @ w_out), ln_mix_g, ln_mix_b)
    h = layer_norm(x) * (1 + sc2) + sh2
    u, gt = jnp.split(h @ w_up, 2, axis=-1)
    x = layer_norm(ALPHA * x + g2 * ((jax.nn.silu(gt) * u) @ w_down), ln_ffn_g, ln_ffn_b)
    return x, k_win, v_win, s_new


def setup_inputs(seed: int = 0) -> dict:
    key = jax.random.key(seed)
    ks = jax.random.split(key, 24)
    L = DEPTH

    def nrm(k, shape, scale):
        return jax.random.normal(k, shape, jnp.float32) * scale

    return {
        'x_prompt': nrm(ks[0], (BATCH, SEQ, D_MODEL), 1.0),
        'x_sample': nrm(ks[1], (DEC_BATCH, DEC_SEQ, D_MODEL), 1.0),
        'cache_attn_k': nrm(ks[2], (L, DEC_BATCH, WINDOW, N_KV_HEADS, HEAD_DIM), 1.0),
        'cache_attn_v': nrm(ks[3], (L, DEC_BATCH, WINDOW, N_KV_HEADS, HEAD_DIM), 1.0),
        'state_hgrn': nrm(ks[4], (L, DEC_BATCH, HGRN_HEADS, HGRN_DK, HGRN_DV), 0.5),
        'c_prompt': nrm(ks[5], (BATCH, D_MODEL), 1.0),
        'c_sample': nrm(ks[6], (DEC_BATCH, D_MODEL), 1.0),
        'w_ada': nrm(ks[7], (L, D_MODEL, 6 * D_MODEL), 0.5 * D_MODEL ** -0.5),
        'b_ada': nrm(ks[8], (L, 6 * D_MODEL), 0.02),
        'w_in': nrm(ks[9], (L, D_MODEL, IN_WIDTH), D_MODEL ** -0.5),
        'attn_sinks': nrm(ks[10], (L, N_Q_HEADS), 1.0),
        'hgrn_lb_logits': nrm(ks[11], (L + 1, HGRN_WIDTH), 0.5),
        'hgrn_norm_w': 1.0 + nrm(ks[12], (L, HGRN_WIDTH), 0.02),
        'w_branch_attn': nrm(ks[13], (L, ATTN_WIDTH, D_MODEL), BETA * ATTN_WIDTH ** -0.5),
        'w_branch_hgrn': nrm(ks[14], (L, HGRN_WIDTH, D_MODEL), BETA * HGRN_WIDTH ** -0.5),
        'w_out': nrm(ks[15], (L, D_MODEL, D_MODEL), BETA * D_MODEL ** -0.5),
        'ln_mix_g': 1.0 + nrm(ks[16], (L, D_MODEL), 0.02),
        'ln_mix_b': nrm(ks[17], (L, D_MODEL), 0.02),
        'w_up': nrm(ks[18], (L, D_MODEL, 2 * D_FF), D_MODEL ** -0.5),
        'w_down': nrm(ks[19], (L, D_FF, D_MODEL), BETA * D_FF ** -0.5),
        'ln_ffn_g': 1.0 + nrm(ks[20], (L, D_MODEL), 0.02),
        'ln_ffn_b': nrm(ks[21], (L, D_MODEL), 0.02),
    }


def reference(x_prompt, x_sample, cache_attn_k, cache_attn_v, state_hgrn, c_prompt, c_sample,
              w_ada, b_ada, w_in, attn_sinks, hgrn_lb_logits, hgrn_norm_w,
              w_branch_attn, w_branch_hgrn, w_out, ln_mix_g, ln_mix_b,
              w_up, w_down, ln_ffn_g, ln_ffn_b):
    lb_all = jnp.cumsum(jax.nn.softmax(hgrn_lb_logits.astype(jnp.float32), axis=0), axis=0)
    xp, xs = x_prompt, x_sample
    kp_l, vp_l, sp_l, ks_l, vs_l, ss_l = [], [], [], [], [], []
    for l in range(DEPTH):
        w = (w_ada[l], b_ada[l], w_in[l], attn_sinks[l], lb_all[l], hgrn_norm_w[l],
             w_branch_attn[l], w_branch_hgrn[l], w_out[l], ln_mix_g[l], ln_mix_b[l],
             w_up[l], w_down[l], ln_ffn_g[l], ln_ffn_b[l])
        xp, kp, vp, sp = layer_forward(xp, c_prompt, None, None, None, *w)
        xs, kn, vn, sn = layer_forward(xs, c_sample, cache_attn_k[l], cache_attn_v[l], state_hgrn[l], *w)
        kp_l.append(kp); vp_l.append(vp); sp_l.append(sp)
        ks_l.append(kn); vs_l.append(vn); ss_l.append(sn)
    new_k_prompt = jnp.stack(kp_l)
    new_v_prompt = jnp.stack(vp_l)
    new_state_prompt = jnp.stack(sp_l)
    new_k_sample = jnp.stack(ks_l)
    new_v_sample = jnp.stack(vs_l)
    new_state_sample = jnp.stack(ss_l)
    return (xp, xs, new_k_prompt, new_v_prompt, new_state_prompt, new_k_sample, new_v_sample, new_state_sample)
```

```python
import functools

import jax
import jax.numpy as jnp
from jax import lax
from jax.experimental import pallas as pl
from jax.experimental.pallas import tpu as pltpu

D_MODEL = 1024
PAST_LEN = 1024
CHUNK = 64
WINDOW = 128
HEAD_DIM = 64
N_Q_HEADS = 8
N_KV_HEADS = 2
Q_GROUP = N_Q_HEADS // N_KV_HEADS
ATTN_WIDTH = N_Q_HEADS * HEAD_DIM
KV_WIDTH = N_KV_HEADS * HEAD_DIM
GROUP_WIDTH = Q_GROUP * HEAD_DIM
HGRN_HEADS = 8
HGRN_DK = 128
HGRN_DV = 128
HGRN_WIDTH = HGRN_HEADS * HGRN_DK
D_FF = 2816
IN_WIDTH = ATTN_WIDTH + 2 * KV_WIDTH + 4 * HGRN_WIDTH + 2 * D_MODEL
DEPTH = 1
ALPHA = (2 * DEPTH) ** 0.25
LN_EPS = 1e-5
RMS_EPS = 1e-6

Q0 = 0
K0 = Q0 + ATTN_WIDTH
V0 = K0 + KV_WIDTH
HQ0 = V0 + KV_WIDTH
HF0 = HQ0 + HGRN_WIDTH
HI0 = HF0 + HGRN_WIDTH
HG0 = HI0 + HGRN_WIDTH
GA0 = HG0 + HGRN_WIDTH
GH0 = GA0 + D_MODEL

LANES = 128
MXU_WIDTH = 256
VMEM_LIMIT_BYTES = 56 * 1024 * 1024

HGRN_SUB = 32
KEY_WIN = 256
BAND_PAD = KEY_WIN - WINDOW - CHUNK
ADA_ROWS = 40
ADA_BLOCK = 512

BF16 = jnp.bfloat16
F32 = jnp.float32


def _sigmoid(x):
    return 1.0 / (1.0 + jnp.exp(-x))


def _ln_rows(x):
    mu = jnp.mean(x, axis=-1, keepdims=True)
    xc = x - mu
    var = jnp.mean(xc * xc, axis=-1, keepdims=True)
    return xc * lax.rsqrt(var + LN_EPS)


def _dot(a, b):
    return jnp.dot(a, b, preferred_element_type=F32)


def _dot_nt(a, b):
    return lax.dot_general(a, b, (((1,), (1,)), ((), ())), preferred_element_type=F32)


def _cumsum_rows(x):
    n = x.shape[0]
    row = lax.broadcasted_iota(jnp.int32, x.shape, 0)
    s = 1
    while s < n:
        x = x + jnp.where(row >= s, pltpu.roll(x, s, axis=0), 0.0)
        s *= 2
    return x


def _ada_kernel(c_ref, w_ref, b_ref, o_ref):
    c = c_ref[...]
    a = c * _sigmoid(c)
    o_ref[...] = _dot(a.astype(BF16), w_ref[...].astype(BF16)) + b_ref[...]


def _ada_call(c_all, w_ada, b_ada):
    n = w_ada.shape[1]
    return pl.pallas_call(
        _ada_kernel,
        grid=(n // ADA_BLOCK,),
        in_specs=[
            pl.BlockSpec((ADA_ROWS, D_MODEL), lambda j: (0, 0)),
            pl.BlockSpec((D_MODEL, ADA_BLOCK), lambda j: (0, j)),
            pl.BlockSpec((1, ADA_BLOCK), lambda j: (0, j)),
        ],
        out_specs=pl.BlockSpec((ADA_ROWS, ADA_BLOCK), lambda j: (0, j)),
        out_shape=jax.ShapeDtypeStruct((ADA_ROWS, n), F32),
        compiler_params=pltpu.CompilerParams(dimension_semantics=("arbitrary",)),
        name="ada_mod",
    )(c_all, w_ada, b_ada)


def _mixer_kernel(x_ref, mod_ref, ck_ref, cv_ref, st_ref, w_in_ref, wba_ref, wbh_ref, wout_ref,
                  sinks_ref, lbl_ref, nw_ref, lng_ref, lnb_ref,
                  y_ref, kw_ref, vw_ref, sn_ref,
                  h_sc, proj_sc, kb_sc, vb_sc, s_sc, ya_sc, yh_sc, *, SB, TT, pos0):
    t = pl.program_id(1)
    last_t = pl.num_programs(1) - 1
    n_chunks = TT // CHUNK
    band_rows = WINDOW + TT

    for s in range(SB):
        sh1 = mod_ref[s, 0:1, :]
        sc1 = mod_ref[s, 1:2, :]
        h = _ln_rows(x_ref[s]) * (1.0 + sc1) + sh1
        h_sc[s * TT:(s + 1) * TT, :] = h.astype(BF16)

    for j in range(IN_WIDTH // MXU_WIDTH):
        cols = slice(j * MXU_WIDTH, (j + 1) * MXU_WIDTH)
        proj_sc[:, cols] = _dot(h_sc[...], w_in_ref[:, cols])

    lane = lax.broadcasted_iota(jnp.int32, (1, LANES), 1)
    lane_g = lax.broadcasted_iota(jnp.int32, (1, GROUP_WIDTH), 1)
    head_of_lane = lane_g >> 6
    row = lax.broadcasted_iota(jnp.int32, (Q_GROUP * CHUNK, 1), 0)
    head_of_row = row >> 6
    q_of_row = row & (CHUNK - 1)
    key = lax.broadcasted_iota(jnp.int32, (1, KEY_WIN), 1)
    dist = jnp.abs(q_of_row + WINDOW - key).astype(F32)

    for s in range(SB):
        r0 = s * TT

        @pl.when(t == 0)
        def _():
            kb_sc[s, 0:WINDOW, :] = ck_ref[s]
            vb_sc[s, 0:WINDOW, :] = cv_ref[s]
            for hh in range(HGRN_HEADS):
                s_sc[s, hh] = st_ref[s, hh].T

        kb_sc[s, WINDOW:band_rows, :] = proj_sc[r0:r0 + TT, K0:K0 + KV_WIDTH]
        vb_sc[s, WINDOW:band_rows, :] = proj_sc[r0:r0 + TT, V0:V0 + KV_WIDTH]
        kb_sc[s, band_rows:, :] = jnp.zeros((BAND_PAD, KV_WIDTH), F32)
        vb_sc[s, band_rows:, :] = jnp.zeros((BAND_PAD, KV_WIDTH), F32)
        kw_ref[s] = kb_sc[s, TT:TT + WINDOW, :]
        vw_ref[s] = vb_sc[s, TT:TT + WINDOW, :]

        kband = kb_sc[s]
        vband = vb_sc[s]
        kroll = pltpu.roll(kband, HEAD_DIM, axis=1)
        vroll = pltpu.roll(vband, HEAD_DIM, axis=1)
        first_half = lane < HEAD_DIM
        for g in range(N_KV_HEADS):
            k1 = jnp.where(first_half, kband, kroll) if g == 0 else jnp.where(first_half, kroll, kband)
            v1 = jnp.where(first_half, vband, vroll) if g == 0 else jnp.where(first_half, vroll, vband)
            kk = jnp.concatenate([k1, k1], axis=1).astype(BF16)
            vv = jnp.concatenate([v1, v1], axis=1).astype(BF16)
            slope = jnp.zeros((Q_GROUP * CHUNK, 1), F32)
            sink = jnp.zeros((Q_GROUP * CHUNK, 1), F32)
            for hq in range(Q_GROUP):
                slope = jnp.where(head_of_row == hq, 2.0 ** -(g * Q_GROUP + hq + 1), slope)
                sink = jnp.where(head_of_row == hq, sinks_ref[g * Q_GROUP + hq], sink)
            bias = slope * dist
            for c in range(n_chunks):
                rq = r0 + c * CHUNK
                qg = proj_sc[rq:rq + CHUNK, g * GROUP_WIDTH:(g + 1) * GROUP_WIDTH] * (HEAD_DIM ** -0.5)
                lhs = jnp.concatenate(
                    [jnp.where(head_of_lane == hq, qg, 0.0) for hq in range(Q_GROUP)], axis=0).astype(BF16)
                kwin = kk[c * CHUNK:c * CHUNK + KEY_WIN]
                vwin = vv[c * CHUNK:c * CHUNK + KEY_WIN]
                sc = _dot_nt(lhs, kwin) - bias
                ok = key < WINDOW + CHUNK
                if pos0 < WINDOW:
                    ok = ok & (key >= WINDOW - pos0 - c * CHUNK - t * TT)
                sc = jnp.where(ok, sc, -jnp.inf)
                m = jnp.maximum(jnp.max(sc, axis=-1, keepdims=True), sink)
                e = jnp.exp(sc - m)
                den = jnp.sum(e, axis=-1, keepdims=True) + jnp.exp(sink - m)
                r = _dot(e.astype(BF16), vwin) / den
                out = r[0:CHUNK]
                for hq in range(1, Q_GROUP):
                    out = jnp.where(head_of_lane == hq, r[hq * CHUNK:(hq + 1) * CHUNK], out)
                ya_sc[rq:rq + CHUNK, g * GROUP_WIDTH:(g + 1) * GROUP_WIDTH] = out.astype(BF16)

        knext = kb_sc[s, TT:TT + WINDOW, :]
        vnext = vb_sc[s, TT:TT + WINDOW, :]
        kb_sc[s, 0:WINDOW, :] = knext
        vb_sc[s, 0:WINDOW, :] = vnext

    l0 = lbl_ref[0:1, :]
    l1 = lbl_ref[1:2, :]
    lmax = jnp.maximum(l0, l1)
    e0 = jnp.exp(l0 - lmax)
    lb = e0 / (e0 + jnp.exp(l1 - lmax))
    oml = 1.0 - lb
    nw = nw_ref[...]
    ri = lax.broadcasted_iota(jnp.int32, (HGRN_SUB, HGRN_SUB), 0)
    ci = lax.broadcasted_iota(jnp.int32, (HGRN_SUB, HGRN_SUB), 1)
    causal = ri >= ci
    mid = HGRN_SUB // 2 - 1

    for s in range(SB):
        def sub_block(i, carry, s=s):
            rows = pl.ds(pl.multiple_of(s * TT + i * HGRN_SUB, HGRN_SUB), HGRN_SUB)
            z = proj_sc[rows, HF0:HF0 + HGRN_WIDTH]
            sg = _sigmoid(z)
            log_f = jnp.log(lb + oml * sg)
            k_in = oml * (1.0 - sg)
            b = _cumsum_rows(log_f)
            b_ref_pt = b[mid:mid + 1, :]
            b_last = b[HGRN_SUB - 1:HGRN_SUB, :]
            hq = proj_sc[rows, HQ0:HQ0 + HGRN_WIDTH]
            qe = hq * _sigmoid(hq) * jnp.exp(b - b_ref_pt)
            ke = k_in * jnp.exp(b_ref_pt - b)
            qi = qe * jnp.exp(b_ref_pt)
            kd = ke * jnp.exp(b_last - b_ref_pt)
            dec = jnp.exp(b_last)
            v = proj_sc[rows, HI0:HI0 + HGRN_WIDTH]
            v_t = v.T
            qe16, ke16, qi16, kd16, v16, vt16 = (a.astype(BF16) for a in (qe, ke, qi, kd, v, v_t))
            outs = []
            for hh in range(HGRN_HEADS):
                sl = slice(hh * HGRN_DK, (hh + 1) * HGRN_DK)
                a = jnp.where(causal, _dot_nt(qe16[:, sl], ke16[:, sl]), 0.0)
                st = s_sc[s, hh]
                o = _dot(a.astype(BF16), v16[:, sl]) + _dot_nt(qi16[:, sl], st.astype(BF16))
                s_sc[s, hh] = dec[:, sl] * st + _dot(vt16[sl, :], kd16[:, sl])
                outs.append(o * lax.rsqrt(jnp.mean(o * o, axis=-1, keepdims=True) + RMS_EPS))
            hg = proj_sc[rows, HG0:HG0 + HGRN_WIDTH]
            y = jnp.concatenate(outs, axis=1) * nw * (hg * _sigmoid(hg))
            yh_sc[rows, :] = y.astype(BF16)
            return carry

        lax.fori_loop(0, TT // HGRN_SUB, sub_block, 0)

        @pl.when(t == last_t)
        def _():
            for hh in range(HGRN_HEADS):
                sn_ref[s, hh] = s_sc[s, hh].T

    pa = _dot(ya_sc[...], wba_ref[...])
    ph = _dot(yh_sc[...], wbh_ref[...])
    merged = (_sigmoid(proj_sc[:, GA0:GA0 + D_MODEL]) * pa
              + _sigmoid(proj_sc[:, GH0:GH0 + D_MODEL]) * ph)
    mo = _dot(merged.astype(BF16), wout_ref[...])
    for s in range(SB):
        g1 = mod_ref[s, 2:3, :]
        u = ALPHA * x_ref[s] + g1 * mo[s * TT:(s + 1) * TT]
        y_ref[s] = _ln_rows(u) * lng_ref[...] + lnb_ref[...]


def _const_spec(shape):
    zeros = (0,) * len(shape)
    return pl.BlockSpec(shape, lambda i, t: zeros, pipeline_mode=pl.Buffered(1))


def _mixer_call(x, mod, ck, cv, st, w_in, wba, wbh, wout, sinks, lbl, nw, lng, lnb, *, SB, TT, pos0):
    nseq, T, _ = x.shape
    TM = SB * TT
    grid = (nseq // SB, T // TT)
    seq_map3 = lambda i, t: (i, 0, 0)
    seq_map4 = lambda i, t: (i, 0, 0, 0)
    kernel = functools.partial(_mixer_kernel, SB=SB, TT=TT, pos0=pos0)
    return pl.pallas_call(
        kernel,
        grid=grid,
        in_specs=[
            pl.BlockSpec((SB, TT, D_MODEL), lambda i, t: (i, t, 0)),
            pl.BlockSpec((SB, 6, D_MODEL), seq_map3),
            pl.BlockSpec((SB, WINDOW, KV_WIDTH), seq_map3),
            pl.BlockSpec((SB, WINDOW, KV_WIDTH), seq_map3),
            pl.BlockSpec((SB, HGRN_HEADS, HGRN_DK, HGRN_DV), seq_map4),
            _const_spec((D_MODEL, IN_WIDTH)),
            _const_spec((ATTN_WIDTH, D_MODEL)),
            _const_spec((HGRN_WIDTH, D_MODEL)),
            _const_spec((D_MODEL, D_MODEL)),
            pl.BlockSpec(memory_space=pltpu.SMEM),
            _const_spec((2, HGRN_WIDTH)),
            _const_spec((1, HGRN_WIDTH)),
            _const_spec((1, D_MODEL)),
            _const_spec((1, D_MODEL)),
        ],
        out_specs=[
            pl.BlockSpec((SB, TT, D_MODEL), lambda i, t: (i, t, 0)),
            pl.BlockSpec((SB, WINDOW, KV_WIDTH), seq_map3),
            pl.BlockSpec((SB, WINDOW, KV_WIDTH), seq_map3),
            pl.BlockSpec((SB, HGRN_HEADS, HGRN_DK, HGRN_DV), seq_map4),
        ],
        out_shape=[
            jax.ShapeDtypeStruct((nseq, T, D_MODEL), F32),
            jax.ShapeDtypeStruct((nseq, WINDOW, KV_WIDTH), F32),
            jax.ShapeDtypeStruct((nseq, WINDOW, KV_WIDTH), F32),
            jax.ShapeDtypeStruct((nseq, HGRN_HEADS, HGRN_DK, HGRN_DV), F32),
        ],
        scratch_shapes=[
            pltpu.VMEM((TM, D_MODEL), BF16),
            pltpu.VMEM((TM, IN_WIDTH), F32),
            pltpu.VMEM((SB, WINDOW + TT + BAND_PAD, KV_WIDTH), F32),
            pltpu.VMEM((SB, WINDOW + TT + BAND_PAD, KV_WIDTH), F32),
            pltpu.VMEM((SB, HGRN_HEADS, HGRN_DV, HGRN_DK), F32),
            pltpu.VMEM((TM, ATTN_WIDTH), BF16),
            pltpu.VMEM((TM, HGRN_WIDTH), BF16),
        ],
        compiler_params=pltpu.CompilerParams(
            dimension_semantics=("arbitrary", "arbitrary"), vmem_limit_bytes=VMEM_LIMIT_BYTES),
        name="mixer",
    )(x, mod, ck, cv, st, w_in, wba, wbh, wout, sinks, lbl, nw, lng, lnb)


def _ffn_kernel(x_ref, mod_ref, wup_ref, wdn_ref, lng_ref, lnb_ref, y_ref, h_sc, act_sc, *, SB, TT):
    for s in range(SB):
        sh2 = mod_ref[s, 3:4, :]
        sc2 = mod_ref[s, 4:5, :]
        h = _ln_rows(x_ref[s]) * (1.0 + sc2) + sh2
        h_sc[s * TT:(s + 1) * TT, :] = h.astype(BF16)
    for j in range(D_FF // MXU_WIDTH):
        cols = slice(j * MXU_WIDTH, (j + 1) * MXU_WIDTH)
        gcols = slice(D_FF + j * MXU_WIDTH, D_FF + (j + 1) * MXU_WIDTH)
        u = _dot(h_sc[...], wup_ref[:, cols])
        gt = _dot(h_sc[...], wup_ref[:, gcols])
        act_sc[:, cols] = (gt * _sigmoid(gt) * u).astype(BF16)
    f = _dot(act_sc[...], wdn_ref[...])
    for s in range(SB):
        g2 = mod_ref[s, 5:6, :]
        u = ALPHA * x_ref[s] + g2 * f[s * TT:(s + 1) * TT]
        y_ref[s] = _ln_rows(u) * lng_ref[...] + lnb_ref[...]


def _ffn_call(x, mod, wup, wdn, lng, lnb, *, SB, TT):
    nseq, T, _ = x.shape
    TM = SB * TT
    kernel = functools.partial(_ffn_kernel, SB=SB, TT=TT)
    return pl.pallas_call(
        kernel,
        grid=(nseq // SB, T // TT),
        in_specs=[
            pl.BlockSpec((SB, TT, D_MODEL), lambda i, t: (i, t, 0)),
            pl.BlockSpec((SB, 6, D_MODEL), lambda i, t: (i, 0, 0)),
            _const_spec((D_MODEL, 2 * D_FF)),
            _const_spec((D_FF, D_MODEL)),
            _const_spec((1, D_MODEL)),
            _const_spec((1, D_MODEL)),
        ],
        out_specs=pl.BlockSpec((SB, TT, D_MODEL), lambda i, t: (i, t, 0)),
        out_shape=jax.ShapeDtypeStruct((nseq, T, D_MODEL), F32),
        scratch_shapes=[
            pltpu.VMEM((TM, D_MODEL), BF16),
            pltpu.VMEM((TM, D_FF), BF16),
        ],
        compiler_params=pltpu.CompilerParams(
            dimension_semantics=("arbitrary", "arbitrary"), vmem_limit_bytes=VMEM_LIMIT_BYTES),
        name="ffn",
    )(x, mod, wup, wdn, lng, lnb)


def kernel(x_prompt, x_sample, cache_attn_k, cache_attn_v, state_hgrn, c_prompt, c_sample, w_ada, b_ada, w_in, attn_sinks, hgrn_lb_logits, hgrn_norm_w, w_branch_attn, w_branch_hgrn, w_out, ln_mix_g, ln_mix_b, w_up, w_down, ln_ffn_g, ln_ffn_b):
    assert w_ada.shape[0] == DEPTH and hgrn_lb_logits.shape[0] == DEPTH + 1
    nb = x_prompt.shape[0]
    ns = x_sample.shape[0]
    assert nb + ns <= ADA_ROWS and x_sample.shape[1] == CHUNK

    c_all = jnp.concatenate([c_prompt, c_sample, jnp.zeros((ADA_ROWS - nb - ns, D_MODEL), F32)], axis=0)
    mod = _ada_call(c_all, w_ada[0], b_ada[0][None, :]).reshape(ADA_ROWS, 6, D_MODEL)
    mod_p = mod[:nb]
    mod_s = mod[nb:nb + ns]

    mixer_w = (w_in[0].astype(BF16), w_branch_attn[0].astype(BF16), w_branch_hgrn[0].astype(BF16),
               w_out[0].astype(BF16), attn_sinks[0], hgrn_lb_logits, hgrn_norm_w[0][None, :],
               ln_mix_g[0][None, :], ln_mix_b[0][None, :])
    ffn_w = (w_up[0].astype(BF16), w_down[0].astype(BF16), ln_ffn_g[0][None, :], ln_ffn_b[0][None, :])

    zk = jnp.zeros((nb, WINDOW, KV_WIDTH), F32)
    zs = jnp.zeros((nb, HGRN_HEADS, HGRN_DK, HGRN_DV), F32)
    x1p, kp, vp, sp = _mixer_call(x_prompt, mod_p, zk, zk, zs, *mixer_w, SB=1, TT=256, pos0=0)
    x1s, ks, vs, ss = _mixer_call(
        x_sample, mod_s,
        cache_attn_k[0].reshape(ns, WINDOW, KV_WIDTH), cache_attn_v[0].reshape(ns, WINDOW, KV_WIDTH),
        state_hgrn[0], *mixer_w, SB=4, TT=CHUNK, pos0=PAST_LEN)

    yp = _ffn_call(x1p, mod_p, *ffn_w, SB=1, TT=512)
    ys = _ffn_call(x1s, mod_s, *ffn_w, SB=8, TT=CHUNK)

    win = lambda a: a.reshape(1, a.shape[0], WINDOW, N_KV_HEADS, HEAD_DIM)
    return (yp, ys, win(kp), win(vp), sp[None], win(ks), win(vs), ss[None])
```

```python
import functools

import jax
import jax.numpy as jnp
from jax import lax
from jax.experimental import pallas as pl
from jax.experimental.pallas import tpu as pltpu

D_MODEL = 1024
PAST_LEN = 1024
CHUNK = 64
WINDOW = 128
HEAD_DIM = 64
N_Q_HEADS = 8
N_KV_HEADS = 2
Q_GROUP = N_Q_HEADS // N_KV_HEADS
ATTN_WIDTH = N_Q_HEADS * HEAD_DIM
KV_WIDTH = N_KV_HEADS * HEAD_DIM
GROUP_WIDTH = Q_GROUP * HEAD_DIM
HGRN_HEADS = 8
HGRN_DK = 128
HGRN_DV = 128
HGRN_WIDTH = HGRN_HEADS * HGRN_DK
D_FF = 2816
IN_WIDTH = ATTN_WIDTH + 2 * KV_WIDTH + 4 * HGRN_WIDTH + 2 * D_MODEL
DEPTH = 1
ALPHA = (2 * DEPTH) ** 0.25
LN_EPS = 1e-5
RMS_EPS = 1e-6

Q0 = 0
K0 = Q0 + ATTN_WIDTH
V0 = K0 + KV_WIDTH
HQ0 = V0 + KV_WIDTH
HF0 = HQ0 + HGRN_WIDTH
HI0 = HF0 + HGRN_WIDTH
HG0 = HI0 + HGRN_WIDTH
GA0 = HG0 + HGRN_WIDTH
GH0 = GA0 + D_MODEL

LANES = 128
MXU_WIDTH = 256
VMEM_LIMIT_BYTES = 56 * 1024 * 1024

HGRN_BLOCK = 64
HGRN_HALF = HGRN_BLOCK // 2
KEY_WIN = 256
BAND_PAD = KEY_WIN - WINDOW - CHUNK
ADA_ROWS = 40
ADA_BLOCK = 512

BF16 = jnp.bfloat16
F32 = jnp.float32


def _sigmoid(x):
    return 1.0 / (1.0 + jnp.exp(-x))


def _ln_rows(x):
    mu = jnp.mean(x, axis=-1, keepdims=True)
    xc = x - mu
    var = jnp.mean(xc * xc, axis=-1, keepdims=True)
    return xc * lax.rsqrt(var + LN_EPS)


def _dot(a, b):
    return jnp.dot(a, b, preferred_element_type=F32)


def _dot_nt(a, b):
    return lax.dot_general(a, b, (((1,), (1,)), ((), ())), preferred_element_type=F32)


def _cumsum_rows(x):
    n = x.shape[0]
    row = lax.broadcasted_iota(jnp.int32, x.shape, 0)
    s = 1
    while s < n:
        x = x + jnp.where(row >= s, pltpu.roll(x, s, axis=0), 0.0)
        s *= 2
    return x


def _ada_kernel(c_ref, w_ref, b_ref, o_ref):
    c = c_ref[...]
    a = c * _sigmoid(c)
    o_ref[...] = _dot(a.astype(BF16), w_ref[...].astype(BF16)) + b_ref[...]


def _ada_call(c_all, w_ada, b_ada):
    n = w_ada.shape[1]
    return pl.pallas_call(
        _ada_kernel,
        grid=(n // ADA_BLOCK,),
        in_specs=[
            pl.BlockSpec((ADA_ROWS, D_MODEL), lambda j: (0, 0)),
            pl.BlockSpec((D_MODEL, ADA_BLOCK), lambda j: (0, j)),
            pl.BlockSpec((1, ADA_BLOCK), lambda j: (0, j)),
        ],
        out_specs=pl.BlockSpec((ADA_ROWS, ADA_BLOCK), lambda j: (0, j)),
        out_shape=jax.ShapeDtypeStruct((ADA_ROWS, n), F32),
        compiler_params=pltpu.CompilerParams(dimension_semantics=("arbitrary",)),
        name="ada_mod",
    )(c_all, w_ada, b_ada)


def _mixer_kernel(x_ref, mod_ref, ck_ref, cv_ref, st_ref, w_in_ref, wba_ref, wbh_ref, wout_ref,
                  sinks_ref, lbl_ref, nw_ref, lng_ref, lnb_ref,
                  y_ref, kw_ref, vw_ref, sn_ref,
                  h_sc, proj_sc, kb_sc, vb_sc, s_sc, ya_sc, yh_sc,
                  l_sc, r_sc, qi_sc, kd_sc, v_sc, vt_sc, dec_sc, on_sc, *, SB, TT, pos0):
    t = pl.program_id(1)
    last_t = pl.num_programs(1) - 1
    n_chunks = TT // CHUNK
    band_rows = WINDOW + TT
    TM = SB * TT

    for s in range(SB):
        sh1 = mod_ref[s, 0:1, :]
        sc1 = mod_ref[s, 1:2, :]
        h = _ln_rows(x_ref[s]) * (1.0 + sc1) + sh1
        h_sc[s * TT:(s + 1) * TT, :] = h.astype(BF16)

    def project(j):
        cols = slice(j * MXU_WIDTH, (j + 1) * MXU_WIDTH)
        proj_sc[:, cols] = _dot(h_sc[...], w_in_ref[:, cols])

    hgrn_first = range(HQ0 // MXU_WIDTH, HG0 // MXU_WIDTH)
    for j in hgrn_first:
        project(j)

    l0 = lbl_ref[0:1, :]
    l1 = lbl_ref[1:2, :]
    lmax = jnp.maximum(l0, l1)
    e0 = jnp.exp(l0 - lmax)
    lb = e0 / (e0 + jnp.exp(l1 - lmax))
    oml = 1.0 - lb
    in_h1 = lax.broadcasted_iota(jnp.int32, (HGRN_BLOCK, 1), 0) < HGRN_HALF
    m1 = HGRN_HALF // 2 - 1
    m2 = HGRN_HALF + m1
    for c in range(TM // HGRN_BLOCK):
        rows = slice(c * HGRN_BLOCK, (c + 1) * HGRN_BLOCK)
        sg = _sigmoid(proj_sc[rows, HF0:HF0 + HGRN_WIDTH])
        log_f = jnp.log(lb + oml * sg)
        k_in = oml * (1.0 - sg)
        b = _cumsum_rows(log_f)
        r1 = b[m1:m1 + 1, :]
        r2 = b[m2:m2 + 1, :]
        b_last = b[HGRN_BLOCK - 1:HGRN_BLOCK, :]
        rh = jnp.where(in_h1, r1, r2)
        hq = proj_sc[rows, HQ0:HQ0 + HGRN_WIDTH]
        qe = hq * _sigmoid(hq) * jnp.exp(b - rh)
        ke = k_in * jnp.exp(rh - b)
        qi = qe * jnp.where(in_h1, jnp.exp(r1), jnp.exp(r2))
        kd = ke * jnp.where(in_h1, jnp.exp(b_last - r1), jnp.exp(b_last - r2))
        la = jnp.where(in_h1, qe, qe * jnp.exp(r2 - r1))
        lb2 = jnp.where(in_h1, 0.0, qe)
        ra = jnp.where(in_h1, ke, 0.0)
        rb = jnp.where(in_h1, 0.0, ke)
        v = proj_sc[rows, HI0:HI0 + HGRN_WIDTH]
        dec = jnp.exp(b_last)
        la16, lb16, ra16, rb16, qi16, kd16, v16, vt16 = (
            a.astype(BF16) for a in (la, lb2, ra, rb, qi, kd, v, v.T))
        for hh in range(HGRN_HEADS):
            sl = slice(hh * HGRN_DK, (hh + 1) * HGRN_DK)
            l_sc[hh, rows, 0:HGRN_DK] = la16[:, sl]
            l_sc[hh, rows, HGRN_DK:2 * HGRN_DK] = lb16[:, sl]
            r_sc[hh, rows, 0:HGRN_DK] = ra16[:, sl]
            r_sc[hh, rows, HGRN_DK:2 * HGRN_DK] = rb16[:, sl]
            qi_sc[hh, rows, :] = qi16[:, sl]
            kd_sc[hh, rows, :] = kd16[:, sl]
            v_sc[hh, rows, :] = v16[:, sl]
            vt_sc[hh, c] = vt16[sl, :]
            dec_sc[hh, c] = dec[:, sl]

    for j in range(IN_WIDTH // MXU_WIDTH):
        if j not in hgrn_first:
            project(j)

    lane = lax.broadcasted_iota(jnp.int32, (1, LANES), 1)
    lane_g = lax.broadcasted_iota(jnp.int32, (1, GROUP_WIDTH), 1)
    head_of_lane = lane_g >> 6
    row = lax.broadcasted_iota(jnp.int32, (Q_GROUP * CHUNK, 1), 0)
    head_of_row = row >> 6
    q_of_row = row & (CHUNK - 1)
    key = lax.broadcasted_iota(jnp.int32, (1, KEY_WIN), 1)
    dist = jnp.abs(q_of_row + WINDOW - key).astype(F32)

    for s in range(SB):
        r0 = s * TT

        @pl.when(t == 0)
        def _():
            kb_sc[s, 0:WINDOW, :] = ck_ref[s]
            vb_sc[s, 0:WINDOW, :] = cv_ref[s]
            for hh in range(HGRN_HEADS):
                s_sc[s, hh] = st_ref[s, hh].T

        kb_sc[s, WINDOW:band_rows, :] = proj_sc[r0:r0 + TT, K0:K0 + KV_WIDTH]
        vb_sc[s, WINDOW:band_rows, :] = proj_sc[r0:r0 + TT, V0:V0 + KV_WIDTH]
        kb_sc[s, band_rows:, :] = jnp.zeros((BAND_PAD, KV_WIDTH), F32)
        vb_sc[s, band_rows:, :] = jnp.zeros((BAND_PAD, KV_WIDTH), F32)
        kw_ref[s] = kb_sc[s, TT:TT + WINDOW, :]
        vw_ref[s] = vb_sc[s, TT:TT + WINDOW, :]

        kband = kb_sc[s]
        vband = vb_sc[s]
        kroll = pltpu.roll(kband, HEAD_DIM, axis=1)
        vroll = pltpu.roll(vband, HEAD_DIM, axis=1)
        first_half = lane < HEAD_DIM
        for g in range(N_KV_HEADS):
            k1 = jnp.where(first_half, kband, kroll) if g == 0 else jnp.where(first_half, kroll, kband)
            v1 = jnp.where(first_half, vband, vroll) if g == 0 else jnp.where(first_half, vroll, vband)
            kk = jnp.concatenate([k1, k1], axis=1).astype(BF16)
            vv = jnp.concatenate([v1, v1], axis=1).astype(BF16)
            slope = jnp.zeros((Q_GROUP * CHUNK, 1), F32)
            sink = jnp.zeros((Q_GROUP * CHUNK, 1), F32)
            for hq in range(Q_GROUP):
                slope = jnp.where(head_of_row == hq, 2.0 ** -(g * Q_GROUP + hq + 1), slope)
                sink = jnp.where(head_of_row == hq, sinks_ref[g * Q_GROUP + hq], sink)
            bias = slope * dist
            for c in range(n_chunks):
                rq = r0 + c * CHUNK
                qg = proj_sc[rq:rq + CHUNK, g * GROUP_WIDTH:(g + 1) * GROUP_WIDTH] * (HEAD_DIM ** -0.5)
                lhs = jnp.concatenate(
                    [jnp.where(head_of_lane == hq, qg, 0.0) for hq in range(Q_GROUP)], axis=0).astype(BF16)
                kwin = kk[c * CHUNK:c * CHUNK + KEY_WIN]
                vwin = vv[c * CHUNK:c * CHUNK + KEY_WIN]
                sc = _dot_nt(lhs, kwin) - bias
                ok = key < WINDOW + CHUNK
                if pos0 < WINDOW:
                    ok = ok & (key >= WINDOW - pos0 - c * CHUNK - t * TT)
                sc = jnp.where(ok, sc, -jnp.inf)
                m = jnp.maximum(jnp.max(sc, axis=-1, keepdims=True), sink)
                e = jnp.exp(sc - m)
                den = jnp.sum(e, axis=-1, keepdims=True) + jnp.exp(sink - m)
                r = _dot(e.astype(BF16), vwin) / den
                out = r[0:CHUNK]
                for hq in range(1, Q_GROUP):
                    out = jnp.where(head_of_lane == hq, r[hq * CHUNK:(hq + 1) * CHUNK], out)
                ya_sc[rq:rq + CHUNK, g * GROUP_WIDTH:(g + 1) * GROUP_WIDTH] = out.astype(BF16)

        knext = kb_sc[s, TT:TT + WINDOW, :]
        vnext = vb_sc[s, TT:TT + WINDOW, :]
        kb_sc[s, 0:WINDOW, :] = knext
        vb_sc[s, 0:WINDOW, :] = vnext

    ri = lax.broadcasted_iota(jnp.int32, (HGRN_BLOCK, HGRN_BLOCK), 0)
    ci = lax.broadcasted_iota(jnp.int32, (HGRN_BLOCK, HGRN_BLOCK), 1)
    causal = ri >= ci
    blocks_per_seq = TT // HGRN_BLOCK

    def head_body(hh, carry):
        for s in range(SB):
            st = s_sc[s, hh]
            for cc in range(blocks_per_seq):
                c = s * blocks_per_seq + cc
                rows = slice(c * HGRN_BLOCK, (c + 1) * HGRN_BLOCK)
                a = jnp.where(causal, _dot_nt(l_sc[hh, rows, :], r_sc[hh, rows, :]), 0.0)
                o = _dot(a.astype(BF16), v_sc[hh, rows, :]) + _dot_nt(qi_sc[hh, rows, :], st.astype(BF16))
                st = dec_sc[hh, c] * st + _dot(vt_sc[hh, c], kd_sc[hh, rows, :])
                on_sc[hh, rows, :] = o * lax.rsqrt(jnp.mean(o * o, axis=-1, keepdims=True) + RMS_EPS)
            s_sc[s, hh] = st
        return carry

    for hh in range(HGRN_HEADS):
        head_body(hh, 0)

    nw = nw_ref[...]
    for hh in range(HGRN_HEADS):
        sl = slice(hh * HGRN_DV, (hh + 1) * HGRN_DV)
        hg = proj_sc[:, HG0 + hh * HGRN_DV:HG0 + (hh + 1) * HGRN_DV]
        yh_sc[:, sl] = (on_sc[hh] * nw[:, sl] * (hg * _sigmoid(hg))).astype(BF16)

    @pl.when(t == last_t)
    def _():
        for s in range(SB):
            for hh in range(HGRN_HEADS):
                sn_ref[s, hh] = s_sc[s, hh].T

    pa = _dot(ya_sc[...], wba_ref[...])
    ph = _dot(yh_sc[...], wbh_ref[...])
    merged = (_sigmoid(proj_sc[:, GA0:GA0 + D_MODEL]) * pa
              + _sigmoid(proj_sc[:, GH0:GH0 + D_MODEL]) * ph)
    mo = _dot(merged.astype(BF16), wout_ref[...])
    for s in range(SB):
        g1 = mod_ref[s, 2:3, :]
        u = ALPHA * x_ref[s] + g1 * mo[s * TT:(s + 1) * TT]
        y_ref[s] = _ln_rows(u) * lng_ref[...] + lnb_ref[...]


def _const_spec(shape):
    zeros = (0,) * len(shape)
    return pl.BlockSpec(shape, lambda i, t: zeros, pipeline_mode=pl.Buffered(1))


def _mixer_call(x, mod, ck, cv, st, w_in, wba, wbh, wout, sinks, lbl, nw, lng, lnb, *, SB, TT, pos0):
    nseq, T, _ = x.shape
    TM = SB * TT
    grid = (nseq // SB, T // TT)
    seq_map3 = lambda i, t: (i, 0, 0)
    seq_map4 = lambda i, t: (i, 0, 0, 0)
    kernel = functools.partial(_mixer_kernel, SB=SB, TT=TT, pos0=pos0)
    return pl.pallas_call(
        kernel,
        grid=grid,
        in_specs=[
            pl.BlockSpec((SB, TT, D_MODEL), lambda i, t: (i, t, 0)),
            pl.BlockSpec((SB, 6, D_MODEL), seq_map3),
            pl.BlockSpec((SB, WINDOW, KV_WIDTH), seq_map3),
            pl.BlockSpec((SB, WINDOW, KV_WIDTH), seq_map3),
            pl.BlockSpec((SB, HGRN_HEADS, HGRN_DK, HGRN_DV), seq_map4),
            _const_spec((D_MODEL, IN_WIDTH)),
            _const_spec((ATTN_WIDTH, D_MODEL)),
            _const_spec((HGRN_WIDTH, D_MODEL)),
            _const_spec((D_MODEL, D_MODEL)),
            pl.BlockSpec(memory_space=pltpu.SMEM),
            _const_spec((2, HGRN_WIDTH)),
            _const_spec((1, HGRN_WIDTH)),
            _const_spec((1, D_MODEL)),
            _const_spec((1, D_MODEL)),
        ],
        out_specs=[
            pl.BlockSpec((SB, TT, D_MODEL), lambda i, t: (i, t, 0)),
            pl.BlockSpec((SB, WINDOW, KV_WIDTH), seq_map3),
            pl.BlockSpec((SB, WINDOW, KV_WIDTH), seq_map3),
            pl.BlockSpec((SB, HGRN_HEADS, HGRN_DK, HGRN_DV), seq_map4),
        ],
        out_shape=[
            jax.ShapeDtypeStruct((nseq, T, D_MODEL), F32),
            jax.ShapeDtypeStruct((nseq, WINDOW, KV_WIDTH), F32),
            jax.ShapeDtypeStruct((nseq, WINDOW, KV_WIDTH), F32),
            jax.ShapeDtypeStruct((nseq, HGRN_HEADS, HGRN_DK, HGRN_DV), F32),
        ],
        scratch_shapes=[
            pltpu.VMEM((TM, D_MODEL), BF16),
            pltpu.VMEM((TM, IN_WIDTH), F32),
            pltpu.VMEM((SB, WINDOW + TT + BAND_PAD, KV_WIDTH), F32),
            pltpu.VMEM((SB, WINDOW + TT + BAND_PAD, KV_WIDTH), F32),
            pltpu.VMEM((SB, HGRN_HEADS, HGRN_DV, HGRN_DK), F32),
            pltpu.VMEM((TM, ATTN_WIDTH), BF16),
            pltpu.VMEM((TM, HGRN_WIDTH), BF16),
            pltpu.VMEM((HGRN_HEADS, TM, 2 * HGRN_DK), BF16),
            pltpu.VMEM((HGRN_HEADS, TM, 2 * HGRN_DK), BF16),
            pltpu.VMEM((HGRN_HEADS, TM, HGRN_DK), BF16),
            pltpu.VMEM((HGRN_HEADS, TM, HGRN_DK), BF16),
            pltpu.VMEM((HGRN_HEADS, TM, HGRN_DV), BF16),
            pltpu.VMEM((HGRN_HEADS, TM // HGRN_BLOCK, HGRN_DV, HGRN_BLOCK), BF16),
            pltpu.VMEM((HGRN_HEADS, TM // HGRN_BLOCK, 1, HGRN_DK), F32),
            pltpu.VMEM((HGRN_HEADS, TM, HGRN_DV), F32),
        ],
        compiler_params=pltpu.CompilerParams(
            dimension_semantics=("arbitrary", "arbitrary"), vmem_limit_bytes=VMEM_LIMIT_BYTES),
        name="mixer",
    )(x, mod, ck, cv, st, w_in, wba, wbh, wout, sinks, lbl, nw, lng, lnb)


def _ffn_kernel(x_ref, mod_ref, wup_ref, wdn_ref, lng_ref, lnb_ref, y_ref, h_sc, act_sc, *, SB, TT):
    for s in range(SB):
        sh2 = mod_ref[s, 3:4, :]
        sc2 = mod_ref[s, 4:5, :]
        h = _ln_rows(x_ref[s]) * (1.0 + sc2) + sh2
        h_sc[s * TT:(s + 1) * TT, :] = h.astype(BF16)
    for j in range(D_FF // MXU_WIDTH):
        cols = slice(j * MXU_WIDTH, (j + 1) * MXU_WIDTH)
        gcols = slice(D_FF + j * MXU_WIDTH, D_FF + (j + 1) * MXU_WIDTH)
        u = _dot(h_sc[...], wup_ref[:, cols])
        gt = _dot(h_sc[...], wup_ref[:, gcols])
        act_sc[:, cols] = (gt * _sigmoid(gt) * u).astype(BF16)
    f = _dot(act_sc[...], wdn_ref[...])
    for s in range(SB):
        g2 = mod_ref[s, 5:6, :]
        u = ALPHA * x_ref[s] + g2 * f[s * TT:(s + 1) * TT]
        y_ref[s] = _ln_rows(u) * lng_ref[...] + lnb_ref[...]


def _ffn_call(x, mod, wup, wdn, lng, lnb, *, SB, TT):
    nseq, T, _ = x.shape
    TM = SB * TT
    kernel = functools.partial(_ffn_kernel, SB=SB, TT=TT)
    return pl.pallas_call(
        kernel,
        grid=(nseq // SB, T // TT),
        in_specs=[
            pl.BlockSpec((SB, TT, D_MODEL), lambda i, t: (i, t, 0)),
            pl.BlockSpec((SB, 6, D_MODEL), lambda i, t: (i, 0, 0)),
            _const_spec((D_MODEL, 2 * D_FF)),
            _const_spec((D_FF, D_MODEL)),
            _const_spec((1, D_MODEL)),
            _const_spec((1, D_MODEL)),
        ],
        out_specs=pl.BlockSpec((SB, TT, D_MODEL), lambda i, t: (i, t, 0)),
        out_shape=jax.ShapeDtypeStruct((nseq, T, D_MODEL), F32),
        scratch_shapes=[
            pltpu.VMEM((TM, D_MODEL), BF16),
            pltpu.VMEM((TM, D_FF), BF16),
        ],
        compiler_params=pltpu.CompilerParams(
            dimension_semantics=("arbitrary", "arbitrary"), vmem_limit_bytes=VMEM_LIMIT_BYTES),
        name="ffn",
    )(x, mod, wup, wdn, lng, lnb)


def kernel(x_prompt, x_sample, cache_attn_k, cache_attn_v, state_hgrn, c_prompt, c_sample, w_ada, b_ada, w_in, attn_sinks, hgrn_lb_logits, hgrn_norm_w, w_branch_attn, w_branch_hgrn, w_out, ln_mix_g, ln_mix_b, w_up, w_down, ln_ffn_g, ln_ffn_b):
    assert w_ada.shape[0] == DEPTH and hgrn_lb_logits.shape[0] == DEPTH + 1
    nb = x_prompt.shape[0]
    ns = x_sample.shape[0]
    assert nb + ns <= ADA_ROWS and x_sample.shape[1] == CHUNK

    c_all = jnp.concatenate([c_prompt, c_sample, jnp.zeros((ADA_ROWS - nb - ns, D_MODEL), F32)], axis=0)
    mod = _ada_call(c_all, w_ada[0], b_ada[0][None, :]).reshape(ADA_ROWS, 6, D_MODEL)
    mod_p = mod[:nb]
    mod_s = mod[nb:nb + ns]

    mixer_w = (w_in[0].astype(BF16), w_branch_attn[0].astype(BF16), w_branch_hgrn[0].astype(BF16),
               w_out[0].astype(BF16), attn_sinks[0], hgrn_lb_logits, hgrn_norm_w[0][None, :],
               ln_mix_g[0][None, :], ln_mix_b[0][None, :])
    ffn_w = (w_up[0].astype(BF16), w_down[0].astype(BF16), ln_ffn_g[0][None, :], ln_ffn_b[0][None, :])

    zk = jnp.zeros((nb, WINDOW, KV_WIDTH), F32)
    zs = jnp.zeros((nb, HGRN_HEADS, HGRN_DK, HGRN_DV), F32)
    x1p, kp, vp, sp = _mixer_call(x_prompt, mod_p, zk, zk, zs, *mixer_w, SB=1, TT=256, pos0=0)
    x1s, ks, vs, ss = _mixer_call(
        x_sample, mod_s,
        cache_attn_k[0].reshape(ns, WINDOW, KV_WIDTH), cache_attn_v[0].reshape(ns, WINDOW, KV_WIDTH),
        state_hgrn[0], *mixer_w, SB=4, TT=CHUNK, pos0=PAST_LEN)

    yp = _ffn_call(x1p, mod_p, *ffn_w, SB=1, TT=512)
    ys = _ffn_call(x1s, mod_s, *ffn_w, SB=8, TT=CHUNK)

    win = lambda a: a.reshape(1, a.shape[0], WINDOW, N_KV_HEADS, HEAD_DIM)
    return (yp, ys, win(kp), win(vp), sp[None], win(ks), win(vs), ss[None])
```

```python
import functools

import jax
import jax.numpy as jnp
from jax import lax
from jax.experimental import pallas as pl
from jax.experimental.pallas import tpu as pltpu

D_MODEL = 1024
PAST_LEN = 1024
CHUNK = 64
WINDOW = 128
HEAD_DIM = 64
N_Q_HEADS = 8
N_KV_HEADS = 2
Q_GROUP = N_Q_HEADS // N_KV_HEADS
ATTN_WIDTH = N_Q_HEADS * HEAD_DIM
KV_WIDTH = N_KV_HEADS * HEAD_DIM
GROUP_WIDTH = Q_GROUP * HEAD_DIM
HGRN_HEADS = 8
HGRN_DK = 128
HGRN_DV = 128
HGRN_WIDTH = HGRN_HEADS * HGRN_DK
D_FF = 2816
IN_WIDTH = ATTN_WIDTH + 2 * KV_WIDTH + 4 * HGRN_WIDTH + 2 * D_MODEL
DEPTH = 1
ALPHA = (2 * DEPTH) ** 0.25
LN_EPS = 1e-5
RMS_EPS = 1e-6

Q0 = 0
K0 = Q0 + ATTN_WIDTH
V0 = K0 + KV_WIDTH
HQ0 = V0 + KV_WIDTH
HF0 = HQ0 + HGRN_WIDTH
HI0 = HF0 + HGRN_WIDTH
HG0 = HI0 + HGRN_WIDTH
GA0 = HG0 + HGRN_WIDTH
GH0 = GA0 + D_MODEL

LANES = 128
MXU_WIDTH = 256
VMEM_LIMIT_BYTES = 56 * 1024 * 1024

HGRN_BLOCK = 64
HGRN_HALF = HGRN_BLOCK // 2
KEY_WIN = 256
BAND_PAD = KEY_WIN - WINDOW - CHUNK
ADA_ROWS = 40
ADA_BLOCK = 512

BF16 = jnp.bfloat16
F32 = jnp.float32


def _sigmoid(x):
    return 1.0 / (1.0 + jnp.exp(-x))


def _ln_rows(x):
    mu = jnp.mean(x, axis=-1, keepdims=True)
    xc = x - mu
    var = jnp.mean(xc * xc, axis=-1, keepdims=True)
    return xc * lax.rsqrt(var + LN_EPS)


def _dot(a, b):
    return jnp.dot(a, b, preferred_element_type=F32)


def _dot_nt(a, b):
    return lax.dot_general(a, b, (((1,), (1,)), ((), ())), preferred_element_type=F32)


def _cumsum_rows(x):
    n = x.shape[0]
    row = lax.broadcasted_iota(jnp.int32, x.shape, 0)
    s = 1
    while s < n:
        x = x + jnp.where(row >= s, pltpu.roll(x, s, axis=0), 0.0)
        s *= 2
    return x


def _ada_kernel(c_ref, w_ref, b_ref, o_ref):
    c = c_ref[...]
    a = c * _sigmoid(c)
    o_ref[...] = _dot(a.astype(BF16), w_ref[...].astype(BF16)) + b_ref[...]


def _ada_call(c_all, w_ada, b_ada):
    n = w_ada.shape[1]
    return pl.pallas_call(
        _ada_kernel,
        grid=(n // ADA_BLOCK,),
        in_specs=[
            pl.BlockSpec((ADA_ROWS, D_MODEL), lambda j: (0, 0)),
            pl.BlockSpec((D_MODEL, ADA_BLOCK), lambda j: (0, j)),
            pl.BlockSpec((1, ADA_BLOCK), lambda j: (0, j)),
        ],
        out_specs=pl.BlockSpec((ADA_ROWS, ADA_BLOCK), lambda j: (0, j)),
        out_shape=jax.ShapeDtypeStruct((ADA_ROWS, n), F32),
        compiler_params=pltpu.CompilerParams(dimension_semantics=("arbitrary",)),
        name="ada_mod",
    )(c_all, w_ada, b_ada)


def _mixer_kernel(x_ref, mod_ref, ck_ref, cv_ref, st_ref, w_in_ref, wba_ref, wbh_ref, wout_ref,
                  sinks_ref, lbl_ref, nw_ref, lng_ref, lnb_ref,
                  y_ref, kw_ref, vw_ref, sn_ref,
                  h_sc, pq_sc, ph_sc, pg_sc, pz_sc, kb_sc, vb_sc, s_sc, ya_sc, yh_sc,
                  a_sc, inc_sc, s16_sc, qi_sc, v_sc, dec_sc, on_sc, *, SB, TT, pos0):
    t = pl.program_id(1)
    last_t = pl.num_programs(1) - 1
    n_chunks = TT // CHUNK
    band_rows = WINDOW + TT
    TM = SB * TT

    @pl.when(t == 0)
    def _():
        for s in range(SB):
            kb_sc[s, 0:WINDOW, :] = ck_ref[s]
            vb_sc[s, 0:WINDOW, :] = cv_ref[s]
            for hh in range(HGRN_HEADS):
                s_sc[s, hh] = st_ref[s, hh].T

    for s in range(SB):
        sh1 = mod_ref[s, 0:1, :]
        sc1 = mod_ref[s, 1:2, :]
        h = _ln_rows(x_ref[s]) * (1.0 + sc1) + sh1
        h_sc[s * TT:(s + 1) * TT, :] = h.astype(BF16)

    def project(dst, w_col0, j):
        cols = slice(j * MXU_WIDTH, (j + 1) * MXU_WIDTH)
        wcols = slice(w_col0 + j * MXU_WIDTH, w_col0 + (j + 1) * MXU_WIDTH)
        dst[:, cols] = _dot(h_sc[...], w_in_ref[:, wcols])

    proj_q = [functools.partial(project, pq_sc, Q0, j) for j in range((HQ0 - Q0) // MXU_WIDTH)]
    proj_h = [functools.partial(project, ph_sc, HQ0, j) for j in range((HG0 - HQ0) // MXU_WIDTH)]
    proj_g = [functools.partial(project, pg_sc, HG0, j) for j in range((GA0 - HG0) // MXU_WIDTH)]
    proj_z = [functools.partial(project, pz_sc, GA0, j) for j in range((IN_WIDTH - GA0) // MXU_WIDTH)]

    for p in proj_h:
        p()

    l0 = lbl_ref[0:1, :]
    l1 = lbl_ref[1:2, :]
    lmax = jnp.maximum(l0, l1)
    e0 = jnp.exp(l0 - lmax)
    lb = e0 / (e0 + jnp.exp(l1 - lmax))
    oml = 1.0 - lb
    in_h1 = lax.broadcasted_iota(jnp.int32, (HGRN_BLOCK, 1), 0) < HGRN_HALF
    causal = (lax.broadcasted_iota(jnp.int32, (HGRN_BLOCK, HGRN_BLOCK), 0)
              >= lax.broadcasted_iota(jnp.int32, (HGRN_BLOCK, HGRN_BLOCK), 1))
    m1 = HGRN_HALF // 2 - 1
    m2 = HGRN_HALF + m1

    def prep_block(c):
        rows = slice(c * HGRN_BLOCK, (c + 1) * HGRN_BLOCK)
        sg = _sigmoid(ph_sc[rows, HGRN_WIDTH:2 * HGRN_WIDTH])
        log_f = jnp.log(lb + oml * sg)
        k_in = oml * (1.0 - sg)
        b = _cumsum_rows(log_f)
        r1 = b[m1:m1 + 1, :]
        r2 = b[m2:m2 + 1, :]
        b_last = b[HGRN_BLOCK - 1:HGRN_BLOCK, :]
        rh = jnp.where(in_h1, r1, r2)
        hq = ph_sc[rows, 0:HGRN_WIDTH]
        qe = hq * _sigmoid(hq) * jnp.exp(b - rh)
        ke = k_in * jnp.exp(rh - b)
        qi = qe * jnp.where(in_h1, jnp.exp(r1), jnp.exp(r2))
        kd = ke * jnp.where(in_h1, jnp.exp(b_last - r1), jnp.exp(b_last - r2))
        la = jnp.where(in_h1, qe, qe * jnp.exp(r2 - r1))
        lb2 = jnp.where(in_h1, 0.0, qe)
        ra = jnp.where(in_h1, ke, 0.0)
        rb = jnp.where(in_h1, 0.0, ke)
        v = ph_sc[rows, 2 * HGRN_WIDTH:3 * HGRN_WIDTH]
        dec = jnp.exp(b_last)
        la16, lb16, ra16, rb16, qi16, kd16, v16, vt16 = (
            a.astype(BF16) for a in (la, lb2, ra, rb, qi, kd, v, v.T))
        for hh in range(HGRN_HEADS):
            sl = slice(hh * HGRN_DK, (hh + 1) * HGRN_DK)
            lhs = jnp.concatenate([la16[:, sl], lb16[:, sl]], axis=1)
            rhs = jnp.concatenate([ra16[:, sl], rb16[:, sl]], axis=1)
            a_sc[hh, rows, :] = jnp.where(causal, _dot_nt(lhs, rhs), 0.0).astype(BF16)
            inc_sc[hh, c] = _dot(vt16[sl, :], kd16[:, sl])
            qi_sc[hh, rows, :] = qi16[:, sl]
            v_sc[hh, rows, :] = v16[:, sl]
            dec_sc[hh, c] = dec[:, sl]

    fill = proj_q + proj_g
    n_blocks = TM // HGRN_BLOCK
    for c in range(n_blocks):
        prep_block(c)
        for p in fill[c * len(fill) // n_blocks:(c + 1) * len(fill) // n_blocks]:
            p()

    lane = lax.broadcasted_iota(jnp.int32, (1, LANES), 1)
    lane_g = lax.broadcasted_iota(jnp.int32, (1, GROUP_WIDTH), 1)
    head_of_lane = lane_g >> 6
    row = lax.broadcasted_iota(jnp.int32, (Q_GROUP * CHUNK, 1), 0)
    head_of_row = row >> 6
    q_of_row = row & (CHUNK - 1)
    key = lax.broadcasted_iota(jnp.int32, (1, KEY_WIN), 1)
    dist = jnp.abs(q_of_row + WINDOW - key).astype(F32)
    first_half = lane < HEAD_DIM

    @functools.cache
    def attn_operands(s, g):
        kband = kb_sc[s]
        vband = vb_sc[s]
        kroll = pltpu.roll(kband, HEAD_DIM, axis=1)
        vroll = pltpu.roll(vband, HEAD_DIM, axis=1)
        k1 = jnp.where(first_half, kband, kroll) if g == 0 else jnp.where(first_half, kroll, kband)
        v1 = jnp.where(first_half, vband, vroll) if g == 0 else jnp.where(first_half, vroll, vband)
        kk = jnp.concatenate([k1, k1], axis=1).astype(BF16)
        vv = jnp.concatenate([v1, v1], axis=1).astype(BF16)
        slope = jnp.zeros((Q_GROUP * CHUNK, 1), F32)
        sink = jnp.zeros((Q_GROUP * CHUNK, 1), F32)
        for hq in range(Q_GROUP):
            slope = jnp.where(head_of_row == hq, 2.0 ** -(g * Q_GROUP + hq + 1), slope)
            sink = jnp.where(head_of_row == hq, sinks_ref[g * Q_GROUP + hq], sink)
        return kk, vv, slope * dist, sink

    def attn_chunk(s, g, c):
        kk, vv, bias, sink = attn_operands(s, g)
        rq = s * TT + c * CHUNK
        qg = pq_sc[rq:rq + CHUNK, g * GROUP_WIDTH:(g + 1) * GROUP_WIDTH] * (HEAD_DIM ** -0.5)
        lhs = jnp.concatenate(
            [jnp.where(head_of_lane == hq, qg, 0.0) for hq in range(Q_GROUP)], axis=0).astype(BF16)
        kwin = kk[c * CHUNK:c * CHUNK + KEY_WIN]
        vwin = vv[c * CHUNK:c * CHUNK + KEY_WIN]
        sc = _dot_nt(lhs, kwin) - bias
        ok = key < WINDOW + CHUNK
        if pos0 < WINDOW:
            ok = ok & (key >= WINDOW - pos0 - c * CHUNK - t * TT)
        sc = jnp.where(ok, sc, -jnp.inf)
        m = jnp.maximum(jnp.max(sc, axis=-1, keepdims=True), sink)
        e = jnp.exp(sc - m)
        den = jnp.sum(e, axis=-1, keepdims=True) + jnp.exp(sink - m)
        r = _dot(e.astype(BF16), vwin) / den
        out = r[0:CHUNK]
        for hq in range(1, Q_GROUP):
            out = jnp.where(head_of_lane == hq, r[hq * CHUNK:(hq + 1) * CHUNK], out)
        ya_sc[rq:rq + CHUNK, g * GROUP_WIDTH:(g + 1) * GROUP_WIDTH] = out.astype(BF16)

    for s in range(SB):
        r0 = s * TT
        kb_sc[s, WINDOW:band_rows, :] = pq_sc[r0:r0 + TT, K0:K0 + KV_WIDTH]
        vb_sc[s, WINDOW:band_rows, :] = pq_sc[r0:r0 + TT, V0:V0 + KV_WIDTH]
        kb_sc[s, band_rows:, :] = jnp.zeros((BAND_PAD, KV_WIDTH), F32)
        vb_sc[s, band_rows:, :] = jnp.zeros((BAND_PAD, KV_WIDTH), F32)
        kw_ref[s] = kb_sc[s, TT:TT + WINDOW, :]
        vw_ref[s] = vb_sc[s, TT:TT + WINDOW, :]

    blocks_per_seq = TT // HGRN_BLOCK

    def hgrn_states(hh):
        for s in range(SB):
            st = s_sc[s, hh]
            for cc in range(blocks_per_seq):
                c = s * blocks_per_seq + cc
                s16_sc[hh, c] = st.astype(BF16)
                st = dec_sc[hh, c] * st + inc_sc[hh, c]
            s_sc[s, hh] = st

    def hgrn_outputs(hh):
        for c in range(SB * blocks_per_seq):
            rows = slice(c * HGRN_BLOCK, (c + 1) * HGRN_BLOCK)
            o = _dot(a_sc[hh, rows, :], v_sc[hh, rows, :]) + _dot_nt(qi_sc[hh, rows, :], s16_sc[hh, c])
            on_sc[hh, rows, :] = o * lax.rsqrt(jnp.mean(o * o, axis=-1, keepdims=True) + RMS_EPS)

    attn_units = [(s, g, c) for s in range(SB) for g in range(N_KV_HEADS) for c in range(n_chunks)]
    n_slots = SB * N_KV_HEADS
    for i in range(n_slots):
        for unit in attn_units[i * n_chunks:(i + 1) * n_chunks]:
            attn_chunk(*unit)
        for p in proj_z[i * len(proj_z) // n_slots:(i + 1) * len(proj_z) // n_slots]:
            p()
    for hh in range(HGRN_HEADS):
        hgrn_states(hh)
    for hh in range(HGRN_HEADS):
        hgrn_outputs(hh)

    for s in range(SB):
        knext = kb_sc[s, TT:TT + WINDOW, :]
        vnext = vb_sc[s, TT:TT + WINDOW, :]
        kb_sc[s, 0:WINDOW, :] = knext
        vb_sc[s, 0:WINDOW, :] = vnext

    nw = nw_ref[...]
    for hh in range(HGRN_HEADS):
        sl = slice(hh * HGRN_DV, (hh + 1) * HGRN_DV)
        hg = pg_sc[:, sl]
        yh_sc[:, sl] = (on_sc[hh] * nw[:, sl] * (hg * _sigmoid(hg))).astype(BF16)

    pa = _dot(ya_sc[...], wba_ref[...])
    ph = _dot(yh_sc[...], wbh_ref[...])
    merged = (_sigmoid(pz_sc[:, 0:D_MODEL]) * pa
              + _sigmoid(pz_sc[:, D_MODEL:2 * D_MODEL]) * ph)
    mo = _dot(merged.astype(BF16), wout_ref[...])
    for s in range(SB):
        g1 = mod_ref[s, 2:3, :]
        u = ALPHA * x_ref[s] + g1 * mo[s * TT:(s + 1) * TT]
        y_ref[s] = _ln_rows(u) * lng_ref[...] + lnb_ref[...]

    @pl.when(t == last_t)
    def _():
        for s in range(SB):
            for hh in range(HGRN_HEADS):
                sn_ref[s, hh] = s_sc[s, hh].T


def _const_spec(shape):
    zeros = (0,) * len(shape)
    return pl.BlockSpec(shape, lambda i, t: zeros, pipeline_mode=pl.Buffered(1))


def _mixer_call(x, mod, ck, cv, st, w_in, wba, wbh, wout, sinks, lbl, nw, lng, lnb, *, SB, TT, pos0):
    nseq, T, _ = x.shape
    TM = SB * TT
    grid = (nseq // SB, T // TT)
    seq_map3 = lambda i, t: (i, 0, 0)
    seq_map4 = lambda i, t: (i, 0, 0, 0)
    kernel = functools.partial(_mixer_kernel, SB=SB, TT=TT, pos0=pos0)
    return pl.pallas_call(
        kernel,
        grid=grid,
        in_specs=[
            pl.BlockSpec((SB, TT, D_MODEL), lambda i, t: (i, t, 0)),
            pl.BlockSpec((SB, 6, D_MODEL), seq_map3),
            pl.BlockSpec((SB, WINDOW, KV_WIDTH), seq_map3),
            pl.BlockSpec((SB, WINDOW, KV_WIDTH), seq_map3),
            pl.BlockSpec((SB, HGRN_HEADS, HGRN_DK, HGRN_DV), seq_map4),
            _const_spec((D_MODEL, IN_WIDTH)),
            _const_spec((ATTN_WIDTH, D_MODEL)),
            _const_spec((HGRN_WIDTH, D_MODEL)),
            _const_spec((D_MODEL, D_MODEL)),
            pl.BlockSpec(memory_space=pltpu.SMEM),
            _const_spec((2, HGRN_WIDTH)),
            _const_spec((1, HGRN_WIDTH)),
            _const_spec((1, D_MODEL)),
            _const_spec((1, D_MODEL)),
        ],
        out_specs=[
            pl.BlockSpec((SB, TT, D_MODEL), lambda i, t: (i, t, 0)),
            pl.BlockSpec((SB, WINDOW, KV_WIDTH), seq_map3),
            pl.BlockSpec((SB, WINDOW, KV_WIDTH), seq_map3),
            pl.BlockSpec((SB, HGRN_HEADS, HGRN_DK, HGRN_DV), seq_map4),
        ],
        out_shape=[
            jax.ShapeDtypeStruct((nseq, T, D_MODEL), F32),
            jax.ShapeDtypeStruct((nseq, WINDOW, KV_WIDTH), F32),
            jax.ShapeDtypeStruct((nseq, WINDOW, KV_WIDTH), F32),
            jax.ShapeDtypeStruct((nseq, HGRN_HEADS, HGRN_DK, HGRN_DV), F32),
        ],
        scratch_shapes=[
            pltpu.VMEM((TM, D_MODEL), BF16),
            pltpu.VMEM((TM, HQ0 - Q0), F32),
            pltpu.VMEM((TM, HG0 - HQ0), F32),
            pltpu.VMEM((TM, GA0 - HG0), F32),
            pltpu.VMEM((TM, IN_WIDTH - GA0), F32),
            pltpu.VMEM((SB, WINDOW + TT + BAND_PAD, KV_WIDTH), F32),
            pltpu.VMEM((SB, WINDOW + TT + BAND_PAD, KV_WIDTH), F32),
            pltpu.VMEM((SB, HGRN_HEADS, HGRN_DV, HGRN_DK), F32),
            pltpu.VMEM((TM, ATTN_WIDTH), BF16),
            pltpu.VMEM((TM, HGRN_WIDTH), BF16),
            pltpu.VMEM((HGRN_HEADS, TM, HGRN_BLOCK), BF16),
            pltpu.VMEM((HGRN_HEADS, TM // HGRN_BLOCK, HGRN_DV, HGRN_DK), F32),
            pltpu.VMEM((HGRN_HEADS, TM // HGRN_BLOCK, HGRN_DV, HGRN_DK), BF16),
            pltpu.VMEM((HGRN_HEADS, TM, HGRN_DK), BF16),
            pltpu.VMEM((HGRN_HEADS, TM, HGRN_DV), BF16),
            pltpu.VMEM((HGRN_HEADS, TM // HGRN_BLOCK, 1, HGRN_DK), F32),
            pltpu.VMEM((HGRN_HEADS, TM, HGRN_DV), F32),
        ],
        compiler_params=pltpu.CompilerParams(
            dimension_semantics=("arbitrary", "arbitrary"), vmem_limit_bytes=VMEM_LIMIT_BYTES),
        name="mixer",
    )(x, mod, ck, cv, st, w_in, wba, wbh, wout, sinks, lbl, nw, lng, lnb)


def _ffn_kernel(x_ref, mod_ref, wup_ref, wdn_ref, lng_ref, lnb_ref, y_ref, h_sc, act_sc, *, SB, TT):
    for s in range(SB):
        sh2 = mod_ref[s, 3:4, :]
        sc2 = mod_ref[s, 4:5, :]
        h = _ln_rows(x_ref[s]) * (1.0 + sc2) + sh2
        h_sc[s * TT:(s + 1) * TT, :] = h.astype(BF16)
    for j in range(D_FF // MXU_WIDTH):
        cols = slice(j * MXU_WIDTH, (j + 1) * MXU_WIDTH)
        gcols = slice(D_FF + j * MXU_WIDTH, D_FF + (j + 1) * MXU_WIDTH)
        u = _dot(h_sc[...], wup_ref[:, cols])
        gt = _dot(h_sc[...], wup_ref[:, gcols])
        act_sc[:, cols] = (gt * _sigmoid(gt) * u).astype(BF16)
    f = _dot(act_sc[...], wdn_ref[...])
    for s in range(SB):
        g2 = mod_ref[s, 5:6, :]
        u = ALPHA * x_ref[s] + g2 * f[s * TT:(s + 1) * TT]
        y_ref[s] = _ln_rows(u) * lng_ref[...] + lnb_ref[...]


def _ffn_call(x, mod, wup, wdn, lng, lnb, *, SB, TT):
    nseq, T, _ = x.shape
    TM = SB * TT
    kernel = functools.partial(_ffn_kernel, SB=SB, TT=TT)
    return pl.pallas_call(
        kernel,
        grid=(nseq // SB, T // TT),
        in_specs=[
            pl.BlockSpec((SB, TT, D_MODEL), lambda i, t: (i, t, 0)),
            pl.BlockSpec((SB, 6, D_MODEL), lambda i, t: (i, 0, 0)),
            _const_spec((D_MODEL, 2 * D_FF)),
            _const_spec((D_FF, D_MODEL)),
            _const_spec((1, D_MODEL)),
            _const_spec((1, D_MODEL)),
        ],
        out_specs=pl.BlockSpec((SB, TT, D_MODEL), lambda i, t: (i, t, 0)),
        out_shape=jax.ShapeDtypeStruct((nseq, T, D_MODEL), F32),
        scratch_shapes=[
            pltpu.VMEM((TM, D_MODEL), BF16),
            pltpu.VMEM((TM, D_FF), BF16),
        ],
        compiler_params=pltpu.CompilerParams(
            dimension_semantics=("arbitrary", "arbitrary"), vmem_limit_bytes=VMEM_LIMIT_BYTES),
        name="ffn",
    )(x, mod, wup, wdn, lng, lnb)


def kernel(x_prompt, x_sample, cache_attn_k, cache_attn_v, state_hgrn, c_prompt, c_sample, w_ada, b_ada, w_in, attn_sinks, hgrn_lb_logits, hgrn_norm_w, w_branch_attn, w_branch_hgrn, w_out, ln_mix_g, ln_mix_b, w_up, w_down, ln_ffn_g, ln_ffn_b):
    assert w_ada.shape[0] == DEPTH and hgrn_lb_logits.shape[0] == DEPTH + 1
    nb = x_prompt.shape[0]
    ns = x_sample.shape[0]
    assert nb + ns <= ADA_ROWS and x_sample.shape[1] == CHUNK

    c_all = jnp.concatenate([c_prompt, c_sample, jnp.zeros((ADA_ROWS - nb - ns, D_MODEL), F32)], axis=0)
    mod = _ada_call(c_all, w_ada[0], b_ada[0][None, :]).reshape(ADA_ROWS, 6, D_MODEL)
    mod_p = mod[:nb]
    mod_s = mod[nb:nb + ns]

    mixer_w = (w_in[0].astype(BF16), w_branch_attn[0].astype(BF16), w_branch_hgrn[0].astype(BF16),
               w_out[0].astype(BF16), attn_sinks[0], hgrn_lb_logits, hgrn_norm_w[0][None, :],
               ln_mix_g[0][None, :], ln_mix_b[0][None, :])
    ffn_w = (w_up[0].astype(BF16), w_down[0].astype(BF16), ln_ffn_g[0][None, :], ln_ffn_b[0][None, :])

    zk = jnp.zeros((nb, WINDOW, KV_WIDTH), F32)
    zs = jnp.zeros((nb, HGRN_HEADS, HGRN_DK, HGRN_DV), F32)
    x1p, kp, vp, sp = _mixer_call(x_prompt, mod_p, zk, zk, zs, *mixer_w, SB=1, TT=256, pos0=0)
    x1s, ks, vs, ss = _mixer_call(
        x_sample, mod_s,
        cache_attn_k[0].reshape(ns, WINDOW, KV_WIDTH), cache_attn_v[0].reshape(ns, WINDOW, KV_WIDTH),
        state_hgrn[0], *mixer_w, SB=4, TT=CHUNK, pos0=PAST_LEN)

    yp = _ffn_call(x1p, mod_p, *ffn_w, SB=1, TT=512)
    ys = _ffn_call(x1s, mod_s, *ffn_w, SB=8, TT=CHUNK)

    win = lambda a: a.reshape(1, a.shape[0], WINDOW, N_KV_HEADS, HEAD_DIM)
    return (yp, ys, win(kp), win(vp), sp[None], win(ks), win(vs), ss[None])
```

```python
import functools

import jax
import jax.numpy as jnp
from jax import lax
from jax.experimental import pallas as pl
from jax.experimental.pallas import tpu as pltpu

D_MODEL = 1024
PAST_LEN = 1024
CHUNK = 64
WINDOW = 128
HEAD_DIM = 64
N_Q_HEADS = 8
N_KV_HEADS = 2
Q_GROUP = N_Q_HEADS // N_KV_HEADS
ATTN_WIDTH = N_Q_HEADS * HEAD_DIM
KV_WIDTH = N_KV_HEADS * HEAD_DIM
GROUP_WIDTH = Q_GROUP * HEAD_DIM
HGRN_HEADS = 8
HGRN_DK = 128
HGRN_DV = 128
HGRN_WIDTH = HGRN_HEADS * HGRN_DK
D_FF = 2816
IN_WIDTH = ATTN_WIDTH + 2 * KV_WIDTH + 4 * HGRN_WIDTH + 2 * D_MODEL
DEPTH = 1
ALPHA = (2 * DEPTH) ** 0.25
LN_EPS = 1e-5
RMS_EPS = 1e-6

Q0 = 0
K0 = Q0 + ATTN_WIDTH
V0 = K0 + KV_WIDTH
HQ0 = V0 + KV_WIDTH
HF0 = HQ0 + HGRN_WIDTH
HI0 = HF0 + HGRN_WIDTH
HG0 = HI0 + HGRN_WIDTH
GA0 = HG0 + HGRN_WIDTH
GH0 = GA0 + D_MODEL

LANES = 128
MXU_WIDTH = 256
VMEM_LIMIT_BYTES = 56 * 1024 * 1024

HGRN_BLOCK = 64
HGRN_HALF = HGRN_BLOCK // 2
KEY_WIN = 256
BAND_PAD = KEY_WIN - WINDOW - CHUNK
ADA_ROWS = 40
ADA_BLOCK = 512

BF16 = jnp.bfloat16
F32 = jnp.float32


def _sigmoid(x):
    return 1.0 / (1.0 + jnp.exp(-x))


def _ln_rows(x):
    mu = jnp.mean(x, axis=-1, keepdims=True)
    xc = x - mu
    var = jnp.mean(xc * xc, axis=-1, keepdims=True)
    return xc * lax.rsqrt(var + LN_EPS)


def _dot(a, b):
    return jnp.dot(a, b, preferred_element_type=F32)


def _dot_nt(a, b):
    return lax.dot_general(a, b, (((1,), (1,)), ((), ())), preferred_element_type=F32)


def _cumsum_rows(x):
    n = x.shape[0]
    row = lax.broadcasted_iota(jnp.int32, x.shape, 0)
    s = 1
    while s < n:
        x = x + jnp.where(row >= s, pltpu.roll(x, s, axis=0), 0.0)
        s *= 2
    return x


def _ada_kernel(c_ref, w_ref, b_ref, o_ref):
    c = c_ref[...]
    a = c * _sigmoid(c)
    o_ref[...] = _dot(a.astype(BF16), w_ref[...].astype(BF16)) + b_ref[...]


def _ada_call(c_all, w_ada, b_ada):
    n = w_ada.shape[1]
    return pl.pallas_call(
        _ada_kernel,
        grid=(n // ADA_BLOCK,),
        in_specs=[
            pl.BlockSpec((ADA_ROWS, D_MODEL), lambda j: (0, 0)),
            pl.BlockSpec((D_MODEL, ADA_BLOCK), lambda j: (0, j)),
            pl.BlockSpec((1, ADA_BLOCK), lambda j: (0, j)),
        ],
        out_specs=pl.BlockSpec((ADA_ROWS, ADA_BLOCK), lambda j: (0, j)),
        out_shape=jax.ShapeDtypeStruct((ADA_ROWS, n), F32),
        compiler_params=pltpu.CompilerParams(dimension_semantics=("arbitrary",)),
        name="ada_mod",
    )(c_all, w_ada, b_ada)


def _mixer_kernel(x_ref, mod_ref, ck_ref, cv_ref, st_ref, w_in_ref, wba_ref, wbh_ref, wout_ref,
                  sinks_ref, lbl_ref, nw_ref, lng_ref, lnb_ref,
                  y_ref, kw_ref, vw_ref, sn_ref,
                  h_sc, pq_sc, ph_sc, pg_sc, pz_sc, kb_sc, vb_sc, s_sc, ya_sc, yh_sc,
                  a_sc, s16_sc, qi_sc, v_sc, *, SB, TT, pos0):
    t = pl.program_id(1)
    last_t = pl.num_programs(1) - 1
    n_chunks = TT // CHUNK
    band_rows = WINDOW + TT
    TM = SB * TT
    blocks_per_seq = TT // HGRN_BLOCK
    n_blocks = SB * blocks_per_seq

    @pl.when(t == 0)
    def _():
        for s in range(SB):
            kb_sc[s, 0:WINDOW, :] = ck_ref[s]
            vb_sc[s, 0:WINDOW, :] = cv_ref[s]
            for hh in range(HGRN_HEADS):
                s_sc[s, hh] = st_ref[s, hh].T

    for s in range(SB):
        sh1 = mod_ref[s, 0:1, :]
        sc1 = mod_ref[s, 1:2, :]
        h = _ln_rows(x_ref[s]) * (1.0 + sc1) + sh1
        h_sc[s * TT:(s + 1) * TT, :] = h.astype(BF16)

    def project(dst, w_col0, j):
        cols = slice(j * MXU_WIDTH, (j + 1) * MXU_WIDTH)
        wcols = slice(w_col0 + j * MXU_WIDTH, w_col0 + (j + 1) * MXU_WIDTH)
        dst[:, cols] = _dot(h_sc[...], w_in_ref[:, wcols])

    proj_q = [functools.partial(project, pq_sc, Q0, j) for j in range((HQ0 - Q0) // MXU_WIDTH)]
    proj_h = [functools.partial(project, ph_sc, HQ0, j) for j in range((HG0 - HQ0) // MXU_WIDTH)]
    proj_g = [functools.partial(project, pg_sc, HG0, j) for j in range((GA0 - HG0) // MXU_WIDTH)]
    proj_z = [functools.partial(project, pz_sc, GA0, j) for j in range((IN_WIDTH - GA0) // MXU_WIDTH)]

    for p in proj_h:
        p()

    l0 = lbl_ref[0:1, :]
    l1 = lbl_ref[1:2, :]
    lmax = jnp.maximum(l0, l1)
    e0 = jnp.exp(l0 - lmax)
    lb = e0 / (e0 + jnp.exp(l1 - lmax))
    oml = 1.0 - lb
    in_h1 = lax.broadcasted_iota(jnp.int32, (HGRN_BLOCK, 1), 0) < HGRN_HALF
    causal = (lax.broadcasted_iota(jnp.int32, (HGRN_BLOCK, HGRN_BLOCK), 0)
              >= lax.broadcasted_iota(jnp.int32, (HGRN_BLOCK, HGRN_BLOCK), 1))
    m1 = HGRN_HALF // 2 - 1
    m2 = HGRN_HALF + m1

    def prep_block(c):
        rows = slice(c * HGRN_BLOCK, (c + 1) * HGRN_BLOCK)
        sg = _sigmoid(ph_sc[rows, HGRN_WIDTH:2 * HGRN_WIDTH])
        log_f = jnp.log(lb + oml * sg)
        k_in = oml * (1.0 - sg)
        b = _cumsum_rows(log_f)
        r1 = b[m1:m1 + 1, :]
        r2 = b[m2:m2 + 1, :]
        b_last = b[HGRN_BLOCK - 1:HGRN_BLOCK, :]
        rh = jnp.where(in_h1, r1, r2)
        hq = ph_sc[rows, 0:HGRN_WIDTH]
        qe = hq * _sigmoid(hq) * jnp.exp(b - rh)
        ke = k_in * jnp.exp(rh - b)
        qi = qe * jnp.where(in_h1, jnp.exp(r1), jnp.exp(r2))
        kd = ke * jnp.where(in_h1, jnp.exp(b_last - r1), jnp.exp(b_last - r2))
        la = jnp.where(in_h1, qe, qe * jnp.exp(r2 - r1))
        lb2 = jnp.where(in_h1, 0.0, qe)
        ra = jnp.where(in_h1, ke, 0.0)
        rb = jnp.where(in_h1, 0.0, ke)
        v = ph_sc[rows, 2 * HGRN_WIDTH:3 * HGRN_WIDTH]
        dec = jnp.exp(b_last)
        la16, lb16, ra16, rb16, qi16, kd16, v16, vt16 = (
            a.astype(BF16) for a in (la, lb2, ra, rb, qi, kd, v, v.T))
        s = c // blocks_per_seq
        for hh in range(HGRN_HEADS):
            sl = slice(hh * HGRN_DK, (hh + 1) * HGRN_DK)
            lhs = jnp.concatenate([la16[:, sl], lb16[:, sl]], axis=1)
            rhs = jnp.concatenate([ra16[:, sl], rb16[:, sl]], axis=1)
            a_sc[hh, rows, :] = jnp.where(causal, _dot_nt(lhs, rhs), 0.0).astype(BF16)
            qi_sc[hh, rows, :] = qi16[:, sl]
            v_sc[hh, rows, :] = v16[:, sl]
            st = s_sc[s, hh]
            s16_sc[hh, c] = st.astype(BF16)
            s_sc[s, hh] = dec[:, sl] * st + _dot(vt16[sl, :], kd16[:, sl])

    fill = proj_q + proj_g
    for c in range(n_blocks):
        prep_block(c)
        for p in fill[c * len(fill) // n_blocks:(c + 1) * len(fill) // n_blocks]:
            p()

    lane = lax.broadcasted_iota(jnp.int32, (1, LANES), 1)
    lane_g = lax.broadcasted_iota(jnp.int32, (1, GROUP_WIDTH), 1)
    head_of_lane = lane_g >> 6
    row = lax.broadcasted_iota(jnp.int32, (Q_GROUP * CHUNK, 1), 0)
    head_of_row = row >> 6
    q_of_row = row & (CHUNK - 1)
    key = lax.broadcasted_iota(jnp.int32, (1, KEY_WIN), 1)
    dist = jnp.abs(q_of_row + WINDOW - key).astype(F32)
    first_half = lane < HEAD_DIM

    @functools.cache
    def attn_operands(s, g):
        kband = kb_sc[s]
        vband = vb_sc[s]
        kroll = pltpu.roll(kband, HEAD_DIM, axis=1)
        vroll = pltpu.roll(vband, HEAD_DIM, axis=1)
        k1 = jnp.where(first_half, kband, kroll) if g == 0 else jnp.where(first_half, kroll, kband)
        v1 = jnp.where(first_half, vband, vroll) if g == 0 else jnp.where(first_half, vroll, vband)
        kk = jnp.concatenate([k1, k1], axis=1).astype(BF16)
        vv = jnp.concatenate([v1, v1], axis=1).astype(BF16)
        slope = jnp.zeros((Q_GROUP * CHUNK, 1), F32)
        sink = jnp.zeros((Q_GROUP * CHUNK, 1), F32)
        for hq in range(Q_GROUP):
            slope = jnp.where(head_of_row == hq, 2.0 ** -(g * Q_GROUP + hq + 1), slope)
            sink = jnp.where(head_of_row == hq, sinks_ref[g * Q_GROUP + hq], sink)
        return kk, vv, slope * dist, sink

    def attn_chunk(s, g, c):
        kk, vv, bias, sink = attn_operands(s, g)
        rq = s * TT + c * CHUNK
        qg = pq_sc[rq:rq + CHUNK, g * GROUP_WIDTH:(g + 1) * GROUP_WIDTH] * (HEAD_DIM ** -0.5)
        lhs = jnp.concatenate(
            [jnp.where(head_of_lane == hq, qg, 0.0) for hq in range(Q_GROUP)], axis=0).astype(BF16)
        kwin = kk[c * CHUNK:c * CHUNK + KEY_WIN]
        vwin = vv[c * CHUNK:c * CHUNK + KEY_WIN]
        sc = _dot_nt(lhs, kwin) - bias
        ok = key < WINDOW + CHUNK
        if pos0 < WINDOW:
            ok = ok & (key >= WINDOW - pos0 - c * CHUNK - t * TT)
        sc = jnp.where(ok, sc, -jnp.inf)
        m = jnp.maximum(jnp.max(sc, axis=-1, keepdims=True), sink)
        e = jnp.exp(sc - m)
        den = jnp.sum(e, axis=-1, keepdims=True) + jnp.exp(sink - m)
        r = _dot(e.astype(BF16), vwin) / den
        out = r[0:CHUNK]
        for hq in range(1, Q_GROUP):
            out = jnp.where(head_of_lane == hq, r[hq * CHUNK:(hq + 1) * CHUNK], out)
        ya_sc[rq:rq + CHUNK, g * GROUP_WIDTH:(g + 1) * GROUP_WIDTH] = out.astype(BF16)

    for s in range(SB):
        r0 = s * TT
        kb_sc[s, WINDOW:band_rows, :] = pq_sc[r0:r0 + TT, K0:K0 + KV_WIDTH]
        vb_sc[s, WINDOW:band_rows, :] = pq_sc[r0:r0 + TT, V0:V0 + KV_WIDTH]
        kb_sc[s, band_rows:, :] = jnp.zeros((BAND_PAD, KV_WIDTH), F32)
        vb_sc[s, band_rows:, :] = jnp.zeros((BAND_PAD, KV_WIDTH), F32)
        kw_ref[s] = kb_sc[s, TT:TT + WINDOW, :]
        vw_ref[s] = vb_sc[s, TT:TT + WINDOW, :]

    nw = nw_ref[...]

    def hgrn_outputs(hh):
        sl = slice(hh * HGRN_DV, (hh + 1) * HGRN_DV)
        for c in range(n_blocks):
            rows = slice(c * HGRN_BLOCK, (c + 1) * HGRN_BLOCK)
            o = _dot(a_sc[hh, rows, :], v_sc[hh, rows, :]) + _dot_nt(qi_sc[hh, rows, :], s16_sc[hh, c])
            o = o * lax.rsqrt(jnp.mean(o * o, axis=-1, keepdims=True) + RMS_EPS)
            hg = pg_sc[rows, sl]
            yh_sc[rows, sl] = (o * nw[:, sl] * (hg * _sigmoid(hg))).astype(BF16)

    attn_units = [(s, g, c) for s in range(SB) for g in range(N_KV_HEADS) for c in range(n_chunks)]
    n_slots = SB * N_KV_HEADS
    for i in range(n_slots):
        for unit in attn_units[i * n_chunks:(i + 1) * n_chunks]:
            attn_chunk(*unit)
        for p in proj_z[i * len(proj_z) // n_slots:(i + 1) * len(proj_z) // n_slots]:
            p()
    for hh in range(HGRN_HEADS):
        hgrn_outputs(hh)

    for s in range(SB):
        knext = kb_sc[s, TT:TT + WINDOW, :]
        vnext = vb_sc[s, TT:TT + WINDOW, :]
        kb_sc[s, 0:WINDOW, :] = knext
        vb_sc[s, 0:WINDOW, :] = vnext

    pa = _dot(ya_sc[...], wba_ref[...])
    ph = _dot(yh_sc[...], wbh_ref[...])
    merged = (_sigmoid(pz_sc[:, 0:D_MODEL]) * pa
              + _sigmoid(pz_sc[:, D_MODEL:2 * D_MODEL]) * ph)
    mo = _dot(merged.astype(BF16), wout_ref[...])
    for s in range(SB):
        g1 = mod_ref[s, 2:3, :]
        u = ALPHA * x_ref[s] + g1 * mo[s * TT:(s + 1) * TT]
        y_ref[s] = _ln_rows(u) * lng_ref[...] + lnb_ref[...]

    @pl.when(t == last_t)
    def _():
        for s in range(SB):
            for hh in range(HGRN_HEADS):
                sn_ref[s, hh] = s_sc[s, hh].T


def _const_spec(shape):
    zeros = (0,) * len(shape)
    return pl.BlockSpec(shape, lambda i, t: zeros, pipeline_mode=pl.Buffered(1))


def _mixer_call(x, mod, ck, cv, st, w_in, wba, wbh, wout, sinks, lbl, nw, lng, lnb, *, SB, TT, pos0):
    nseq, T, _ = x.shape
    TM = SB * TT
    grid = (nseq // SB, T // TT)
    seq_map3 = lambda i, t: (i, 0, 0)
    seq_map4 = lambda i, t: (i, 0, 0, 0)
    kernel = functools.partial(_mixer_kernel, SB=SB, TT=TT, pos0=pos0)
    return pl.pallas_call(
        kernel,
        grid=grid,
        in_specs=[
            pl.BlockSpec((SB, TT, D_MODEL), lambda i, t: (i, t, 0)),
            pl.BlockSpec((SB, 6, D_MODEL), seq_map3),
            pl.BlockSpec((SB, WINDOW, KV_WIDTH), seq_map3),
            pl.BlockSpec((SB, WINDOW, KV_WIDTH), seq_map3),
            pl.BlockSpec((SB, HGRN_HEADS, HGRN_DK, HGRN_DV), seq_map4),
            _const_spec((D_MODEL, IN_WIDTH)),
            _const_spec((ATTN_WIDTH, D_MODEL)),
            _const_spec((HGRN_WIDTH, D_MODEL)),
            _const_spec((D_MODEL, D_MODEL)),
            pl.BlockSpec(memory_space=pltpu.SMEM),
            _const_spec((2, HGRN_WIDTH)),
            _const_spec((1, HGRN_WIDTH)),
            _const_spec((1, D_MODEL)),
            _const_spec((1, D_MODEL)),
        ],
        out_specs=[
            pl.BlockSpec((SB, TT, D_MODEL), lambda i, t: (i, t, 0)),
            pl.BlockSpec((SB, WINDOW, KV_WIDTH), seq_map3),
            pl.BlockSpec((SB, WINDOW, KV_WIDTH), seq_map3),
            pl.BlockSpec((SB, HGRN_HEADS, HGRN_DK, HGRN_DV), seq_map4),
        ],
        out_shape=[
            jax.ShapeDtypeStruct((nseq, T, D_MODEL), F32),
            jax.ShapeDtypeStruct((nseq, WINDOW, KV_WIDTH), F32),
            jax.ShapeDtypeStruct((nseq, WINDOW, KV_WIDTH), F32),
            jax.ShapeDtypeStruct((nseq, HGRN_HEADS, HGRN_DK, HGRN_DV), F32),
        ],
        scratch_shapes=[
            pltpu.VMEM((TM, D_MODEL), BF16),
            pltpu.VMEM((TM, HQ0 - Q0), F32),
            pltpu.VMEM((TM, HG0 - HQ0), F32),
            pltpu.VMEM((TM, GA0 - HG0), F32),
            pltpu.VMEM((TM, IN_WIDTH - GA0), F32),
            pltpu.VMEM((SB, WINDOW + TT + BAND_PAD, KV_WIDTH), F32),
            pltpu.VMEM((SB, WINDOW + TT + BAND_PAD, KV_WIDTH), F32),
            pltpu.VMEM((SB, HGRN_HEADS, HGRN_DV, HGRN_DK), F32),
            pltpu.VMEM((TM, ATTN_WIDTH), BF16),
            pltpu.VMEM((TM, HGRN_WIDTH), BF16),
            pltpu.VMEM((HGRN_HEADS, TM, HGRN_BLOCK), BF16),
            pltpu.VMEM((HGRN_HEADS, TM // HGRN_BLOCK, HGRN_DV, HGRN_DK), BF16),
            pltpu.VMEM((HGRN_HEADS, TM, HGRN_DK), BF16),
            pltpu.VMEM((HGRN_HEADS, TM, HGRN_DV), BF16),
        ],
        compiler_params=pltpu.CompilerParams(
            dimension_semantics=("arbitrary", "arbitrary"), vmem_limit_bytes=VMEM_LIMIT_BYTES),
        name="mixer",
    )(x, mod, ck, cv, st, w_in, wba, wbh, wout, sinks, lbl, nw, lng, lnb)


def _ffn_kernel(x_ref, mod_ref, wup_ref, wdn_ref, lng_ref, lnb_ref, y_ref, h_sc, act_sc, *, SB, TT):
    for s in range(SB):
        sh2 = mod_ref[s, 3:4, :]
        sc2 = mod_ref[s, 4:5, :]
        h = _ln_rows(x_ref[s]) * (1.0 + sc2) + sh2
        h_sc[s * TT:(s + 1) * TT, :] = h.astype(BF16)
    for j in range(D_FF // MXU_WIDTH):
        cols = slice(j * MXU_WIDTH, (j + 1) * MXU_WIDTH)
        gcols = slice(D_FF + j * MXU_WIDTH, D_FF + (j + 1) * MXU_WIDTH)
        u = _dot(h_sc[...], wup_ref[:, cols])
        gt = _dot(h_sc[...], wup_ref[:, gcols])
        act_sc[:, cols] = (gt * _sigmoid(gt) * u).astype(BF16)
    f = _dot(act_sc[...], wdn_ref[...])
    for s in range(SB):
        g2 = mod_ref[s, 5:6, :]
        u = ALPHA * x_ref[s] + g2 * f[s * TT:(s + 1) * TT]
        y_ref[s] = _ln_rows(u) * lng_ref[...] + lnb_ref[...]


def _ffn_call(x, mod, wup, wdn, lng, lnb, *, SB, TT):
    nseq, T, _ = x.shape
    TM = SB * TT
    kernel = functools.partial(_ffn_kernel, SB=SB, TT=TT)
    return pl.pallas_call(
        kernel,
        grid=(nseq // SB, T // TT),
        in_specs=[
            pl.BlockSpec((SB, TT, D_MODEL), lambda i, t: (i, t, 0)),
            pl.BlockSpec((SB, 6, D_MODEL), lambda i, t: (i, 0, 0)),
            _const_spec((D_MODEL, 2 * D_FF)),
            _const_spec((D_FF, D_MODEL)),
            _const_spec((1, D_MODEL)),
            _const_spec((1, D_MODEL)),
        ],
        out_specs=pl.BlockSpec((SB, TT, D_MODEL), lambda i, t: (i, t, 0)),
        out_shape=jax.ShapeDtypeStruct((nseq, T, D_MODEL), F32),
        scratch_shapes=[
            pltpu.VMEM((TM, D_MODEL), BF16),
            pltpu.VMEM((TM, D_FF), BF16),
        ],
        compiler_params=pltpu.CompilerParams(
            dimension_semantics=("arbitrary", "arbitrary"), vmem_limit_bytes=VMEM_LIMIT_BYTES),
        name="ffn",
    )(x, mod, wup, wdn, lng, lnb)


def kernel(x_prompt, x_sample, cache_attn_k, cache_attn_v, state_hgrn, c_prompt, c_sample, w_ada, b_ada, w_in, attn_sinks, hgrn_lb_logits, hgrn_norm_w, w_branch_attn, w_branch_hgrn, w_out, ln_mix_g, ln_mix_b, w_up, w_down, ln_ffn_g, ln_ffn_b):
    assert w_ada.shape[0] == DEPTH and hgrn_lb_logits.shape[0] == DEPTH + 1
    nb = x_prompt.shape[0]
    ns = x_sample.shape[0]
    assert nb + ns <= ADA_ROWS and x_sample.shape[1] == CHUNK

    c_all = jnp.concatenate([c_prompt, c_sample, jnp.zeros((ADA_ROWS - nb - ns, D_MODEL), F32)], axis=0)
    mod = _ada_call(c_all, w_ada[0], b_ada[0][None, :]).reshape(ADA_ROWS, 6, D_MODEL)
    mod_p = mod[:nb]
    mod_s = mod[nb:nb + ns]

    mixer_w = (w_in[0].astype(BF16), w_branch_attn[0].astype(BF16), w_branch_hgrn[0].astype(BF16),
               w_out[0].astype(BF16), attn_sinks[0], hgrn_lb_logits, hgrn_norm_w[0][None, :],
               ln_mix_g[0][None, :], ln_mix_b[0][None, :])
    ffn_w = (w_up[0].astype(BF16), w_down[0].astype(BF16), ln_ffn_g[0][None, :], ln_ffn_b[0][None, :])

    zk = jnp.zeros((nb, WINDOW, KV_WIDTH), F32)
    zs = jnp.zeros((nb, HGRN_HEADS, HGRN_DK, HGRN_DV), F32)
    x1p, kp, vp, sp = _mixer_call(x_prompt, mod_p, zk, zk, zs, *mixer_w, SB=1, TT=512, pos0=0)
    x1s, ks, vs, ss = _mixer_call(
        x_sample, mod_s,
        cache_attn_k[0].reshape(ns, WINDOW, KV_WIDTH), cache_attn_v[0].reshape(ns, WINDOW, KV_WIDTH),
        state_hgrn[0], *mixer_w, SB=4, TT=CHUNK, pos0=PAST_LEN)

    yp = _ffn_call(x1p, mod_p, *ffn_w, SB=1, TT=512)
    ys = _ffn_call(x1s, mod_s, *ffn_w, SB=8, TT=CHUNK)

    win = lambda a: a.reshape(1, a.shape[0], WINDOW, N_KV_HEADS, HEAD_DIM)
    return (yp, ys, win(kp), win(vp), sp[None], win(ks), win(vs), ss[None])
```

```python
import functools

import jax
import jax.numpy as jnp
from jax import lax
from jax.experimental import pallas as pl
from jax.experimental.pallas import tpu as pltpu

D_MODEL = 1024
PAST_LEN = 1024
CHUNK = 64
WINDOW = 128
HEAD_DIM = 64
N_Q_HEADS = 8
N_KV_HEADS = 2
Q_GROUP = N_Q_HEADS // N_KV_HEADS
ATTN_WIDTH = N_Q_HEADS * HEAD_DIM
KV_WIDTH = N_KV_HEADS * HEAD_DIM
GROUP_WIDTH = Q_GROUP * HEAD_DIM
HGRN_HEADS = 8
HGRN_DK = 128
HGRN_DV = 128
HGRN_WIDTH = HGRN_HEADS * HGRN_DK
D_FF = 2816
IN_WIDTH = ATTN_WIDTH + 2 * KV_WIDTH + 4 * HGRN_WIDTH + 2 * D_MODEL
DEPTH = 1
ALPHA = (2 * DEPTH) ** 0.25
LN_EPS = 1e-5
RMS_EPS = 1e-6

Q0 = 0
K0 = Q0 + ATTN_WIDTH
V0 = K0 + KV_WIDTH
HQ0 = V0 + KV_WIDTH
HF0 = HQ0 + HGRN_WIDTH
HI0 = HF0 + HGRN_WIDTH
HG0 = HI0 + HGRN_WIDTH
GA0 = HG0 + HGRN_WIDTH
GH0 = GA0 + D_MODEL

LANES = 128
MXU_WIDTH = 256
VMEM_LIMIT_BYTES = 60 * 1024 * 1024

HGRN_BLOCK = 64
HGRN_HALF = HGRN_BLOCK // 2
KEY_WIN = 256
BAND_PAD = KEY_WIN - WINDOW - CHUNK
ADA_ROWS = 40
ADA_BLOCK = 512

BF16 = jnp.bfloat16
F32 = jnp.float32


def _sigmoid(x):
    return 1.0 / (1.0 + jnp.exp(-x))


def _ln_rows(x):
    mu = jnp.mean(x, axis=-1, keepdims=True)
    xc = x - mu
    var = jnp.mean(xc * xc, axis=-1, keepdims=True)
    return xc * lax.rsqrt(var + LN_EPS)


def _dot(a, b):
    return jnp.dot(a, b, preferred_element_type=F32)


def _dot_nt(a, b):
    return lax.dot_general(a, b, (((1,), (1,)), ((), ())), preferred_element_type=F32)


def _cumsum_rows(x):
    n = x.shape[0]
    row = lax.broadcasted_iota(jnp.int32, x.shape, 0)
    s = 1
    while s < n:
        x = x + jnp.where(row >= s, pltpu.roll(x, s, axis=0), 0.0)
        s *= 2
    return x


def _ada_kernel(c_ref, w_ref, b_ref, o_ref):
    c = c_ref[...]
    a = c * _sigmoid(c)
    o_ref[...] = _dot(a.astype(BF16), w_ref[...].astype(BF16)) + b_ref[...]


def _ada_call(c_all, w_ada, b_ada):
    n = w_ada.shape[1]
    return pl.pallas_call(
        _ada_kernel,
        grid=(n // ADA_BLOCK,),
        in_specs=[
            pl.BlockSpec((ADA_ROWS, D_MODEL), lambda j: (0, 0)),
            pl.BlockSpec((D_MODEL, ADA_BLOCK), lambda j: (0, j)),
            pl.BlockSpec((1, ADA_BLOCK), lambda j: (0, j)),
        ],
        out_specs=pl.BlockSpec((ADA_ROWS, ADA_BLOCK), lambda j: (0, j)),
        out_shape=jax.ShapeDtypeStruct((ADA_ROWS, n), F32),
        compiler_params=pltpu.CompilerParams(dimension_semantics=("arbitrary",)),
        name="ada_mod",
    )(c_all, w_ada, b_ada)


def _mixer_kernel(x_ref, mod_ref, ck_ref, cv_ref, st_ref, w_in_ref, wba_ref, wbh_ref, wout_ref,
                  sinks_ref, lbl_ref, nw_ref, lng_ref, lnb_ref,
                  y_ref, kw_ref, vw_ref, sn_ref,
                  h_sc, pq_sc, ph_sc, pg_sc, pz_sc, kb_sc, vb_sc, s_sc, ya_sc, yh_sc,
                  a_sc, s16_sc, qi_sc, v_sc, *, SB, TT, pos0):
    t = pl.program_id(1)
    last_t = pl.num_programs(1) - 1
    n_chunks = TT // CHUNK
    band_rows = WINDOW + TT
    TM = SB * TT
    blocks_per_seq = TT // HGRN_BLOCK
    n_blocks = SB * blocks_per_seq

    @pl.when(t == 0)
    def _():
        for s in range(SB):
            kb_sc[s, 0:WINDOW, :] = ck_ref[s]
            vb_sc[s, 0:WINDOW, :] = cv_ref[s]
            for hh in range(HGRN_HEADS):
                s_sc[s, hh] = st_ref[s, hh]

    for s in range(SB):
        sh1 = mod_ref[s, 0:1, :]
        sc1 = mod_ref[s, 1:2, :]
        h = _ln_rows(x_ref[s]) * (1.0 + sc1) + sh1
        h_sc[s * TT:(s + 1) * TT, :] = h.astype(BF16)

    def project(dst, w_col0, j):
        cols = slice(j * MXU_WIDTH, (j + 1) * MXU_WIDTH)
        wcols = slice(w_col0 + j * MXU_WIDTH, w_col0 + (j + 1) * MXU_WIDTH)
        dst[:, cols] = _dot(h_sc[...], w_in_ref[:, wcols])

    proj_q = [functools.partial(project, pq_sc, Q0, j) for j in range((HQ0 - Q0) // MXU_WIDTH)]
    proj_h = [functools.partial(project, ph_sc, HQ0, j) for j in range((HG0 - HQ0) // MXU_WIDTH)]
    proj_g = [functools.partial(project, pg_sc, HG0, j) for j in range((GA0 - HG0) // MXU_WIDTH)]
    proj_z = [functools.partial(project, pz_sc, GA0, j) for j in range((IN_WIDTH - GA0) // MXU_WIDTH)]

    l0 = lbl_ref[0:1, :]
    l1 = lbl_ref[1:2, :]
    lmax = jnp.maximum(l0, l1)
    e0 = jnp.exp(l0 - lmax)
    lb = e0 / (e0 + jnp.exp(l1 - lmax))
    oml = 1.0 - lb
    in_h1 = lax.broadcasted_iota(jnp.int32, (HGRN_BLOCK, 1), 0) < HGRN_HALF
    causal = (lax.broadcasted_iota(jnp.int32, (HGRN_BLOCK, HGRN_BLOCK), 0)
              >= lax.broadcasted_iota(jnp.int32, (HGRN_BLOCK, HGRN_BLOCK), 1))
    m1 = HGRN_HALF // 2 - 1
    m2 = HGRN_HALF + m1

    def prep_block(c):
        rows = slice(c * HGRN_BLOCK, (c + 1) * HGRN_BLOCK)
        sg = _sigmoid(ph_sc[rows, HGRN_WIDTH:2 * HGRN_WIDTH])
        log_f = jnp.log(lb + oml * sg)
        k_in = oml * (1.0 - sg)
        b = _cumsum_rows(log_f)
        r1 = b[m1:m1 + 1, :]
        r2 = b[m2:m2 + 1, :]
        b_last = b[HGRN_BLOCK - 1:HGRN_BLOCK, :]
        rh = jnp.where(in_h1, r1, r2)
        hq = ph_sc[rows, 0:HGRN_WIDTH]
        qe = hq * _sigmoid(hq) * jnp.exp(b - rh)
        ke = k_in * jnp.exp(rh - b)
        qi = qe * jnp.where(in_h1, jnp.exp(r1), jnp.exp(r2))
        kd = ke * jnp.where(in_h1, jnp.exp(b_last - r1), jnp.exp(b_last - r2))
        la = jnp.where(in_h1, qe, qe * jnp.exp(r2 - r1))
        lb2 = jnp.where(in_h1, 0.0, qe)
        ra = jnp.where(in_h1, ke, 0.0)
        rb = jnp.where(in_h1, 0.0, ke)
        v = ph_sc[rows, 2 * HGRN_WIDTH:3 * HGRN_WIDTH]
        dec_col = jnp.exp(b_last).T
        la16, lb16, ra16, rb16, qi16, kdt16, v16 = (
            a.astype(BF16) for a in (la, lb2, ra, rb, qi, kd.T, v))
        s = c // blocks_per_seq
        for hh in range(HGRN_HEADS):
            sl = slice(hh * HGRN_DK, (hh + 1) * HGRN_DK)
            lhs = jnp.concatenate([la16[:, sl], lb16[:, sl]], axis=1)
            rhs = jnp.concatenate([ra16[:, sl], rb16[:, sl]], axis=1)
            a_sc[hh, rows, :] = jnp.where(causal, _dot_nt(lhs, rhs), 0.0).astype(BF16)
            qi_sc[hh, rows, :] = qi16[:, sl]
            v_sc[hh, rows, :] = v16[:, sl]
            st = s_sc[s, hh]
            s16_sc[hh, c] = st.astype(BF16)
            s_sc[s, hh] = dec_col[sl, :] * st + _dot(kdt16[sl, :], v16[:, sl])

    lane = lax.broadcasted_iota(jnp.int32, (1, LANES), 1)
    row =lax.broadcasted_iota(jnp.int32, (Q_GROUP * CHUNK, 1), 0)
    head_of_row = row >> 6
    q_of_row = row & (CHUNK - 1)
    key = lax.broadcasted_iota(jnp.int32, (1, KEY_WIN), 1)
    dist = jnp.abs(q_of_row + WINDOW - key).astype(F32)
    first_half = lane < HEAD_DIM

    @functools.cache
    def attn_operands(s, g):
        kband = kb_sc[s]
        vband = vb_sc[s]
        kroll = pltpu.roll(kband, HEAD_DIM, axis=1)
        vroll = pltpu.roll(vband, HEAD_DIM, axis=1)
        kk = jnp.where(first_half, kband if g == 0 else kroll, 0.0).astype(BF16)
        v1 = jnp.where(first_half, vband, vroll) if g == 0 else jnp.where(first_half, vroll, vband)
        vv = v1.astype(BF16)
        slope = jnp.zeros((Q_GROUP * CHUNK, 1), F32)
        sink = jnp.zeros((Q_GROUP * CHUNK, 1), F32)
        for hq in range(Q_GROUP):
            slope = jnp.where(head_of_row == hq, 2.0 ** -(g * Q_GROUP + hq + 1), slope)
            sink = jnp.where(head_of_row == hq, sinks_ref[g * Q_GROUP + hq], sink)
        return kk, vv, slope * dist, sink

    def attn_chunk(s, g, c):
        kk, vv, bias, sink = attn_operands(s, g)
        rq = s * TT + c * CHUNK
        qg = pq_sc[rq:rq + CHUNK, g * GROUP_WIDTH:(g + 1) * GROUP_WIDTH] * (HEAD_DIM ** -0.5)
        parts = []
        for j in range(GROUP_WIDTH // LANES):
            pair = qg[:, j * LANES:(j + 1) * LANES]
            parts.append(jnp.where(first_half, pair, 0.0))
            parts.append(jnp.where(first_half, pltpu.roll(pair, HEAD_DIM, axis=1), 0.0))
        lhs = jnp.concatenate(parts, axis=0).astype(BF16)
        kwin = kk[c * CHUNK:c * CHUNK + KEY_WIN]
        vwin = vv[c * CHUNK:c * CHUNK + KEY_WIN]
        sc = _dot_nt(lhs, kwin) - bias
        ok = key < WINDOW + CHUNK
        if pos0 < WINDOW:
            ok = ok & (key >= WINDOW - pos0 - c * CHUNK - t * TT)
        sc = jnp.where(ok, sc, -jnp.inf)
        m = jnp.maximum(jnp.max(sc, axis=-1, keepdims=True), sink)
        e = jnp.exp(sc - m)
        den = jnp.sum(e, axis=-1, keepdims=True) + jnp.exp(sink - m)
        r = _dot(e.astype(BF16), vwin) / den
        out = jnp.concatenate(
            [jnp.where(first_half, r[2 * j * CHUNK:(2 * j + 1) * CHUNK], r[(2 * j + 1) * CHUNK:(2 * j + 2) * CHUNK])
             for j in range(GROUP_WIDTH // LANES)], axis=1)
        ya_sc[rq:rq + CHUNK, g * GROUP_WIDTH:(g + 1) * GROUP_WIDTH] = out.astype(BF16)

    def fill_bands():
        for s in range(SB):
            r0 = s * TT
            kb_sc[s, WINDOW:band_rows, :] = pq_sc[r0:r0 + TT, K0:K0 + KV_WIDTH]
            vb_sc[s, WINDOW:band_rows, :] = pq_sc[r0:r0 + TT, V0:V0 + KV_WIDTH]
            kb_sc[s, band_rows:, :] = jnp.zeros((BAND_PAD, KV_WIDTH), F32)
            vb_sc[s, band_rows:, :] = jnp.zeros((BAND_PAD, KV_WIDTH), F32)
            kw_ref[s] = kb_sc[s, TT:TT + WINDOW, :]
            vw_ref[s] = vb_sc[s, TT:TT + WINDOW, :]

    nw = nw_ref[...]

    def hgrn_outputs(hh):
        sl = slice(hh * HGRN_DV, (hh + 1) * HGRN_DV)
        for c in range(n_blocks):
            rows = slice(c * HGRN_BLOCK, (c + 1) * HGRN_BLOCK)
            o = _dot(a_sc[hh, rows, :], v_sc[hh, rows, :]) + _dot(qi_sc[hh, rows, :], s16_sc[hh, c])
            o = o * lax.rsqrt(jnp.mean(o * o, axis=-1, keepdims=True) + RMS_EPS)
            hg = pg_sc[rows, sl]
            yh_sc[rows, sl] = (o * nw[:, sl] * (hg * _sigmoid(hg))).astype(BF16)

    def interleave(units, fillers):
        for i, unit in enumerate(units):
            unit()
            for p in fillers[i * len(fillers) // len(units):(i + 1) * len(fillers) // len(units)]:
                p()

    def attn_group(s, g):
        for c in range(n_chunks):
            attn_chunk(s, g, c)

    for p in proj_h:
        p()
    interleave([functools.partial(prep_block, c) for c in range(n_blocks)], proj_q + proj_g)
    fill_bands()
    interleave([functools.partial(attn_group, s, g) for s in range(SB) for g in range(N_KV_HEADS)], proj_z)
    for hh in range(HGRN_HEADS):
        hgrn_outputs(hh)

    for s in range(SB):
        knext = kb_sc[s, TT:TT + WINDOW, :]
        vnext = vb_sc[s, TT:TT + WINDOW, :]
        kb_sc[s, 0:WINDOW, :] = knext
        vb_sc[s, 0:WINDOW, :] = vnext

    pa = _dot(ya_sc[...], wba_ref[...])
    ph = _dot(yh_sc[...], wbh_ref[...])
    merged = (_sigmoid(pz_sc[:, 0:D_MODEL]) * pa
              + _sigmoid(pz_sc[:, D_MODEL:2 * D_MODEL]) * ph)
    mo = _dot(merged.astype(BF16), wout_ref[...])
    for s in range(SB):
        g1 = mod_ref[s, 2:3, :]
        u = ALPHA * x_ref[s] + g1 * mo[s * TT:(s + 1) * TT]
        y_ref[s] = _ln_rows(u) * lng_ref[...] + lnb_ref[...]

    @pl.when(t == last_t)
    def _():
        for s in range(SB):
            for hh in range(HGRN_HEADS):
                sn_ref[s, hh] = s_sc[s, hh]


def _const_spec(shape):
    zeros = (0,) * len(shape)
    return pl.BlockSpec(shape, lambda i, t: zeros, pipeline_mode=pl.Buffered(1))


def _mixer_call(x, mod, ck, cv, st, w_in, wba, wbh, wout, sinks, lbl, nw, lng, lnb, *, SB, TT, pos0):
    nseq, T, _ = x.shape
    TM = SB * TT
    grid = (nseq // SB, T // TT)
    seq_map3 = lambda i, t: (i, 0, 0)
    seq_map4 = lambda i, t: (i, 0, 0, 0)
    kernel = functools.partial(_mixer_kernel, SB=SB, TT=TT, pos0=pos0)
    return pl.pallas_call(
        kernel,
        grid=grid,
        in_specs=[
            pl.BlockSpec((SB, TT, D_MODEL), lambda i, t: (i, t, 0)),
            pl.BlockSpec((SB, 6, D_MODEL), seq_map3),
            pl.BlockSpec((SB, WINDOW, KV_WIDTH), seq_map3),
            pl.BlockSpec((SB, WINDOW, KV_WIDTH), seq_map3),
            pl.BlockSpec((SB, HGRN_HEADS, HGRN_DK, HGRN_DV), seq_map4),
            _const_spec((D_MODEL, IN_WIDTH)),
            _const_spec((ATTN_WIDTH, D_MODEL)),
            _const_spec((HGRN_WIDTH, D_MODEL)),
            _const_spec((D_MODEL, D_MODEL)),
            pl.BlockSpec(memory_space=pltpu.SMEM),
            _const_spec((2, HGRN_WIDTH)),
            _const_spec((1, HGRN_WIDTH)),
            _const_spec((1, D_MODEL)),
            _const_spec((1, D_MODEL)),
        ],
        out_specs=[
            pl.BlockSpec((SB, TT, D_MODEL), lambda i, t: (i, t, 0)),
            pl.BlockSpec((SB, WINDOW, KV_WIDTH), seq_map3),
            pl.BlockSpec((SB, WINDOW, KV_WIDTH), seq_map3),
            pl.BlockSpec((SB, HGRN_HEADS, HGRN_DK, HGRN_DV), seq_map4),
        ],
        out_shape=[
            jax.ShapeDtypeStruct((nseq, T, D_MODEL), F32),
            jax.ShapeDtypeStruct((nseq, WINDOW, KV_WIDTH), F32),
            jax.ShapeDtypeStruct((nseq, WINDOW, KV_WIDTH), F32),
            jax.ShapeDtypeStruct((nseq, HGRN_HEADS, HGRN_DK, HGRN_DV), F32),
        ],
        scratch_shapes=[
            pltpu.VMEM((TM, D_MODEL), BF16),
            pltpu.VMEM((TM, HQ0 - Q0), F32),
            pltpu.VMEM((TM, HG0 - HQ0), F32),
            pltpu.VMEM((TM, GA0 - HG0), F32),
            pltpu.VMEM((TM, IN_WIDTH - GA0), F32),
            pltpu.VMEM((SB, WINDOW + TT + BAND_PAD, KV_WIDTH), F32),
            pltpu.VMEM((SB, WINDOW + TT + BAND_PAD, KV_WIDTH), F32),
            pltpu.VMEM((SB, HGRN_HEADS, HGRN_DK, HGRN_DV), F32),
            pltpu.VMEM((TM, ATTN_WIDTH), BF16),
            pltpu.VMEM((TM, HGRN_WIDTH), BF16),
            pltpu.VMEM((HGRN_HEADS, TM, HGRN_BLOCK), BF16),
            pltpu.VMEM((HGRN_HEADS, TM // HGRN_BLOCK, HGRN_DK, HGRN_DV), BF16),
            pltpu.VMEM((HGRN_HEADS, TM, HGRN_DK), BF16),
            pltpu.VMEM((HGRN_HEADS, TM, HGRN_DV), BF16),
        ],
        compiler_params=pltpu.CompilerParams(
            dimension_semantics=("arbitrary", "arbitrary"), vmem_limit_bytes=VMEM_LIMIT_BYTES),
        name="mixer",
    )(x, mod, ck, cv, st, w_in, wba, wbh, wout, sinks, lbl, nw, lng, lnb)


def _ffn_kernel(x_ref, mod_ref, wup_ref, wdn_ref, lng_ref, lnb_ref, y_ref, h_sc, act_sc, *, SB, TT):
    for s in range(SB):
        sh2 = mod_ref[s, 3:4, :]
        sc2 = mod_ref[s, 4:5, :]
        h = _ln_rows(x_ref[s]) * (1.0 + sc2) + sh2
        h_sc[s * TT:(s + 1) * TT, :] = h.astype(BF16)
    for j in range(D_FF // MXU_WIDTH):
        cols = slice(j * MXU_WIDTH, (j + 1) * MXU_WIDTH)
        gcols = slice(D_FF + j * MXU_WIDTH, D_FF + (j + 1) * MXU_WIDTH)
        u = _dot(h_sc[...], wup_ref[:, cols])
        gt = _dot(h_sc[...], wup_ref[:, gcols])
        act_sc[:, cols] = (gt * _sigmoid(gt) * u).astype(BF16)
    f = _dot(act_sc[...], wdn_ref[...])
    for s in range(SB):
        g2 = mod_ref[s, 5:6, :]
        u = ALPHA * x_ref[s] + g2 * f[s * TT:(s + 1) * TT]
        y_ref[s] = _ln_rows(u) * lng_ref[...] + lnb_ref[...]


def _ffn_call(x, mod, wup, wdn, lng, lnb, *, SB, TT):
    nseq, T, _ = x.shape
    TM = SB * TT
    kernel = functools.partial(_ffn_kernel, SB=SB, TT=TT)
    return pl.pallas_call(
        kernel,
        grid=(nseq // SB, T // TT),
        in_specs=[
            pl.BlockSpec((SB, TT, D_MODEL), lambda i, t: (i, t, 0)),
            pl.BlockSpec((SB, 6, D_MODEL), lambda i, t: (i, 0, 0)),
            _const_spec((D_MODEL, 2 * D_FF)),
            _const_spec((D_FF, D_MODEL)),
            _const_spec((1, D_MODEL)),
            _const_spec((1, D_MODEL)),
        ],
        out_specs=pl.BlockSpec((SB, TT, D_MODEL), lambda i, t: (i, t, 0)),
        out_shape=jax.ShapeDtypeStruct((nseq, T, D_MODEL), F32),
        scratch_shapes=[
            pltpu.VMEM((TM, D_MODEL), BF16),
            pltpu.VMEM((TM, D_FF), BF16),
        ],
        compiler_params=pltpu.CompilerParams(
            dimension_semantics=("arbitrary", "arbitrary"), vmem_limit_bytes=VMEM_LIMIT_BYTES),
        name="ffn",
    )(x, mod, wup, wdn, lng, lnb)


def kernel(x_prompt, x_sample, cache_attn_k, cache_attn_v, state_hgrn, c_prompt, c_sample, w_ada, b_ada, w_in, attn_sinks, hgrn_lb_logits, hgrn_norm_w, w_branch_attn, w_branch_hgrn, w_out, ln_mix_g, ln_mix_b, w_up, w_down, ln_ffn_g, ln_ffn_b):
    assert w_ada.shape[0] == DEPTH and hgrn_lb_logits.shape[0] == DEPTH + 1
    nb = x_prompt.shape[0]
    ns = x_sample.shape[0]
    assert nb + ns <= ADA_ROWS and x_sample.shape[1] == CHUNK

    c_all = jnp.concatenate([c_prompt, c_sample, jnp.zeros((ADA_ROWS - nb - ns, D_MODEL), F32)], axis=0)
    mod = _ada_call(c_all, w_ada[0], b_ada[0][None, :]).reshape(ADA_ROWS, 6, D_MODEL)
    mod_p = mod[:nb]
    mod_s = mod[nb:nb + ns]

    mixer_w = (w_in[0].astype(BF16), w_branch_attn[0].astype(BF16), w_branch_hgrn[0].astype(BF16),
               w_out[0].astype(BF16), attn_sinks[0], hgrn_lb_logits, hgrn_norm_w[0][None, :],
               ln_mix_g[0][None, :], ln_mix_b[0][None, :])
    ffn_w = (w_up[0].astype(BF16), w_down[0].astype(BF16), ln_ffn_g[0][None, :], ln_ffn_b[0][None, :])

    zk = jnp.zeros((nb, WINDOW, KV_WIDTH), F32)
    zs = jnp.zeros((nb, HGRN_HEADS, HGRN_DK, HGRN_DV), F32)
    x1p, kp, vp, sp = _mixer_call(x_prompt, mod_p, zk, zk, zs, *mixer_w, SB=1, TT=512, pos0=0)
    x1s, ks, vs, ss = _mixer_call(
        x_sample, mod_s,
        cache_attn_k[0].reshape(ns, WINDOW, KV_WIDTH), cache_attn_v[0].reshape(ns, WINDOW, KV_WIDTH),
        state_hgrn[0], *mixer_w, SB=4, TT=CHUNK, pos0=PAST_LEN)

    yp = _ffn_call(x1p, mod_p, *ffn_w, SB=1, TT=512)
    ys = _ffn_call(x1s, mod_s, *ffn_w, SB=8, TT=CHUNK)

    win = lambda a: a.reshape(1, a.shape[0], WINDOW, N_KV_HEADS, HEAD_DIM)
    return (yp, ys, win(kp), win(vp), sp[None], win(ks), win(vs), ss[None])
```

```python
import functools

import jax
import jax.numpy as jnp
from jax import lax
from jax.experimental import pallas as pl
from jax.experimental.pallas import tpu as pltpu

D_MODEL = 1024
PAST_LEN = 1024
CHUNK = 64
WINDOW = 128
HEAD_DIM = 64
N_Q_HEADS = 8
N_KV_HEADS = 2
Q_GROUP = N_Q_HEADS // N_KV_HEADS
ATTN_WIDTH = N_Q_HEADS * HEAD_DIM
KV_WIDTH = N_KV_HEADS * HEAD_DIM
GROUP_WIDTH = Q_GROUP * HEAD_DIM
HGRN_HEADS = 8
HGRN_DK = 128
HGRN_DV = 128
HGRN_WIDTH = HGRN_HEADS * HGRN_DK
D_FF = 2816
IN_WIDTH = ATTN_WIDTH + 2 * KV_WIDTH + 4 * HGRN_WIDTH + 2 * D_MODEL
DEPTH = 1
ALPHA = (2 * DEPTH) ** 0.25
LN_EPS = 1e-5
RMS_EPS = 1e-6

Q0 = 0
K0 = Q0 + ATTN_WIDTH
V0 = K0 + KV_WIDTH
HQ0 = V0 + KV_WIDTH
HF0 = HQ0 + HGRN_WIDTH
HI0 = HF0 + HGRN_WIDTH
HG0 = HI0 + HGRN_WIDTH
GA0 = HG0 + HGRN_WIDTH
GH0 = GA0 + D_MODEL

LANES = 128
MXU_WIDTH = 256
VMEM_LIMIT_BYTES = 60 * 1024 * 1024

HGRN_BLOCK = 64
HGRN_HALF = HGRN_BLOCK // 2
KEY_WIN = 256
BAND_PAD = KEY_WIN - WINDOW - CHUNK
ADA_ROWS = 40
ADA_BLOCK = 512

BF16 = jnp.bfloat16
F32 = jnp.float32


def _sigmoid(x):
    return 1.0 / (1.0 + jnp.exp(-x))


def _ln_rows(x):
    mu = jnp.mean(x, axis=-1, keepdims=True)
    xc = x - mu
    var = jnp.mean(xc * xc, axis=-1, keepdims=True)
    return xc * lax.rsqrt(var + LN_EPS)


def _dot(a, b):
    return jnp.dot(a, b, preferred_element_type=F32)


def _dot_nt(a, b):
    return lax.dot_general(a, b, (((1,), (1,)), ((), ())), preferred_element_type=F32)


def _cumsum_rows(x):
    n = x.shape[0]
    row = lax.broadcasted_iota(jnp.int32, x.shape, 0)
    s = 1
    while s < n:
        x = x + jnp.where(row >= s, pltpu.roll(x, s, axis=0), 0.0)
        s *= 2
    return x


def _ordered_after(dst, src):
    zero = (lax.bitcast_convert_type(src, jnp.uint32) >> 16) >> 16
    return lax.bitcast_convert_type(lax.bitcast_convert_type(dst, jnp.uint32) + zero, F32)


def _ada_kernel(c_ref, w_ref, b_ref, o_ref):
    c = c_ref[...]
    a = c * _sigmoid(c)
    o_ref[...] = _dot(a.astype(BF16), w_ref[...].astype(BF16)) + b_ref[...]


def _ada_call(c_all, w_ada, b_ada):
    n = w_ada.shape[1]
    return pl.pallas_call(
        _ada_kernel,
        grid=(n // ADA_BLOCK,),
        in_specs=[
            pl.BlockSpec((ADA_ROWS, D_MODEL), lambda j: (0, 0)),
            pl.BlockSpec((D_MODEL, ADA_BLOCK), lambda j: (0, j)),
            pl.BlockSpec((1, ADA_BLOCK), lambda j: (0, j)),
        ],
        out_specs=pl.BlockSpec((ADA_ROWS, ADA_BLOCK), lambda j: (0, j)),
        out_shape=jax.ShapeDtypeStruct((ADA_ROWS, n), F32),
        compiler_params=pltpu.CompilerParams(dimension_semantics=("arbitrary",)),
        name="ada_mod",
    )(c_all, w_ada, b_ada)


def _mixer_kernel(x_ref, mod_ref, ck_ref, cv_ref, st_ref, w_in_ref, wba_ref, wbh_ref, wout_ref,
                  sinks_ref, lbl_ref, nw_ref, lng_ref, lnb_ref,
                  y_ref, kw_ref, vw_ref, sn_ref,
                  h_sc, pq_sc, ph_sc, pg_sc, pz_sc, kb_sc, vb_sc, s_sc, ya_sc, yh_sc,
                  a_sc, s16_sc, qi_sc, v_sc, *, SB, TT, pos0):
    t = pl.program_id(1)
    last_t = pl.num_programs(1) - 1
    n_chunks = TT // CHUNK
    band_rows = WINDOW + TT
    TM = SB * TT
    blocks_per_seq = TT // HGRN_BLOCK
    n_blocks = SB * blocks_per_seq

    @pl.when(t == 0)
    def _():
        for s in range(SB):
            kb_sc[s, 0:WINDOW, :] = ck_ref[s]
            vb_sc[s, 0:WINDOW, :] = cv_ref[s]
            for hh in range(HGRN_HEADS):
                s_sc[s, hh] = st_ref[s, hh]

    for s in range(SB):
        sh1 = mod_ref[s, 0:1, :]
        sc1 = mod_ref[s, 1:2, :]
        h = _ln_rows(x_ref[s]) * (1.0 + sc1) + sh1
        h_sc[s * TT:(s + 1) * TT, :] = h.astype(BF16)

    def project(dst, w_col0, j):
        cols = slice(j * MXU_WIDTH, (j + 1) * MXU_WIDTH)
        wcols = slice(w_col0 + j * MXU_WIDTH, w_col0 + (j + 1) * MXU_WIDTH)
        dst[:, cols] = _dot(h_sc[...], w_in_ref[:, wcols])

    proj_q = [functools.partial(project, pq_sc, Q0, j) for j in range((HQ0 - Q0) // MXU_WIDTH)]
    proj_h = [functools.partial(project, ph_sc, HQ0, j) for j in range((HG0 - HQ0) // MXU_WIDTH)]
    proj_g = [functools.partial(project, pg_sc, HG0, j) for j in range((GA0 - HG0) // MXU_WIDTH)]
    proj_z = [functools.partial(project, pz_sc, GA0, j) for j in range((IN_WIDTH - GA0) // MXU_WIDTH)]

    l0 = lbl_ref[0:1, :]
    l1 = lbl_ref[1:2, :]
    lmax = jnp.maximum(l0, l1)
    e0 = jnp.exp(l0 - lmax)
    lb = e0 / (e0 + jnp.exp(l1 - lmax))
    oml = 1.0 - lb
    in_h1 = lax.broadcasted_iota(jnp.int32, (HGRN_BLOCK, 1), 0) < HGRN_HALF
    causal = (lax.broadcasted_iota(jnp.int32, (HGRN_BLOCK, HGRN_BLOCK), 0)
              >= lax.broadcasted_iota(jnp.int32, (HGRN_BLOCK, HGRN_BLOCK), 1))
    m1 = HGRN_HALF // 2 - 1
    m2 = HGRN_HALF + m1

    def prep_block(c):
        rows = slice(c * HGRN_BLOCK, (c + 1) * HGRN_BLOCK)
        sg = _sigmoid(ph_sc[rows, HGRN_WIDTH:2 * HGRN_WIDTH])
        log_f = jnp.log(lb + oml * sg)
        k_in = oml * (1.0 - sg)
        b = _cumsum_rows(log_f)
        r1 = b[m1:m1 + 1, :]
        r2 = b[m2:m2 + 1, :]
        b_last = b[HGRN_BLOCK - 1:HGRN_BLOCK, :]
        rh = jnp.where(in_h1, r1, r2)
        hq = ph_sc[rows, 0:HGRN_WIDTH]
        qe = hq * _sigmoid(hq) * jnp.exp(b - rh)
        ke = k_in * jnp.exp(rh - b)
        qi = qe * jnp.where(in_h1, jnp.exp(r1), jnp.exp(r2))
        kd = ke * jnp.where(in_h1, jnp.exp(b_last - r1), jnp.exp(b_last - r2))
        la = jnp.where(in_h1, qe, qe * jnp.exp(r2 - r1))
        lb2 = jnp.where(in_h1, 0.0, qe)
        ra = jnp.where(in_h1, ke, 0.0)
        rb = jnp.where(in_h1, 0.0, ke)
        v = ph_sc[rows, 2 * HGRN_WIDTH:3 * HGRN_WIDTH]
        dec_col = jnp.exp(b_last).T
        la16, lb16, ra16, rb16, qi16, kdt16, v16 = (
            a.astype(BF16) for a in (la, lb2, ra, rb, qi, kd.T, v))
        s = c // blocks_per_seq
        for hh in range(HGRN_HEADS):
            sl = slice(hh * HGRN_DK, (hh + 1) * HGRN_DK)
            lhs = jnp.concatenate([la16[:, sl], lb16[:, sl]], axis=1)
            rhs = jnp.concatenate([ra16[:, sl], rb16[:, sl]], axis=1)
            a_sc[hh, rows, :] = jnp.where(causal, _dot_nt(lhs, rhs), 0.0).astype(BF16)
            qi_sc[hh, rows, :] = qi16[:, sl]
            v_sc[hh, rows, :] = v16[:, sl]
            st = s_sc[s, hh]
            s16_sc[hh, c] = st.astype(BF16)
            s_sc[s, hh] = dec_col[sl, :] * st + _dot(kdt16[sl, :], v16[:, sl])

    lane = lax.broadcasted_iota(jnp.int32, (1, LANES), 1)
    row =lax.broadcasted_iota(jnp.int32, (Q_GROUP * CHUNK, 1), 0)
    head_of_row = row >> 6
    q_of_row = row & (CHUNK - 1)
    key = lax.broadcasted_iota(jnp.int32, (1, KEY_WIN), 1)
    dist = jnp.abs(q_of_row + WINDOW - key).astype(F32)
    first_half = lane < HEAD_DIM

    @functools.cache
    def attn_operands(s, g):
        kband = kb_sc[s]
        vband = vb_sc[s]
        kroll = pltpu.roll(kband, HEAD_DIM, axis=1)
        vroll = pltpu.roll(vband, HEAD_DIM, axis=1)
        kk = jnp.where(first_half, kband if g == 0 else kroll, 0.0).astype(BF16)
        v1 = jnp.where(first_half, vband, vroll) if g == 0 else jnp.where(first_half, vroll, vband)
        vv = v1.astype(BF16)
        slope = jnp.zeros((Q_GROUP * CHUNK, 1), F32)
        sink = jnp.zeros((Q_GROUP * CHUNK, 1), F32)
        for hq in range(Q_GROUP):
            slope = jnp.where(head_of_row == hq, 2.0 ** -(g * Q_GROUP + hq + 1), slope)
            sink = jnp.where(head_of_row == hq, sinks_ref[g * Q_GROUP + hq], sink)
        return kk, vv, slope * dist, sink

    def attn_chunk(s, g, c):
        kk, vv, bias, sink = attn_operands(s, g)
        rq = s * TT + c * CHUNK
        qg = pq_sc[rq:rq + CHUNK, g * GROUP_WIDTH:(g + 1) * GROUP_WIDTH] * (HEAD_DIM ** -0.5)
        parts = []
        for j in range(GROUP_WIDTH // LANES):
            pair = qg[:, j * LANES:(j + 1) * LANES]
            parts.append(jnp.where(first_half, pair, 0.0))
            parts.append(jnp.where(first_half, pltpu.roll(pair, HEAD_DIM, axis=1), 0.0))
        lhs = jnp.concatenate(parts, axis=0).astype(BF16)
        kwin = kk[c * CHUNK:c * CHUNK + KEY_WIN]
        vwin = vv[c * CHUNK:c * CHUNK + KEY_WIN]
        sc = _dot_nt(lhs, kwin) - bias
        ok = key < WINDOW + CHUNK
        if pos0 < WINDOW:
            ok = ok & (key >= WINDOW - pos0 - c * CHUNK - t * TT)
        sc = jnp.where(ok, sc, -jnp.inf)
        m = jnp.maximum(jnp.max(sc, axis=-1, keepdims=True), sink)
        e = jnp.exp(sc - m)
        den = jnp.sum(e, axis=-1, keepdims=True) + jnp.exp(sink - m)
        r = _dot(e.astype(BF16), vwin) / den
        out = jnp.concatenate(
            [jnp.where(first_half, r[2 * j * CHUNK:(2 * j + 1) * CHUNK], r[(2 * j + 1) * CHUNK:(2 * j + 2) * CHUNK])
             for j in range(GROUP_WIDTH // LANES)], axis=1)
        ya_sc[rq:rq + CHUNK, g * GROUP_WIDTH:(g + 1) * GROUP_WIDTH] = out.astype(BF16)

    def fill_bands():
        for s in range(SB):
            r0 = s * TT
            kb_sc[s, WINDOW:band_rows, :] = pq_sc[r0:r0 + TT, K0:K0 + KV_WIDTH]
            vb_sc[s, WINDOW:band_rows, :] = pq_sc[r0:r0 + TT, V0:V0 + KV_WIDTH]
            kb_sc[s, band_rows:, :] = jnp.zeros((BAND_PAD, KV_WIDTH), F32)
            vb_sc[s, band_rows:, :] = jnp.zeros((BAND_PAD, KV_WIDTH), F32)
            kw_ref[s] = kb_sc[s, TT:TT + WINDOW, :]
            vw_ref[s] = vb_sc[s, TT:TT + WINDOW, :]

    nw = nw_ref[...]

    def hgrn_outputs(hh):
        sl = slice(hh * HGRN_DV, (hh + 1) * HGRN_DV)
        for c in range(n_blocks):
            rows = slice(c * HGRN_BLOCK, (c + 1) * HGRN_BLOCK)
            o = _dot(a_sc[hh, rows, :], v_sc[hh, rows, :]) + _dot(qi_sc[hh, rows, :], s16_sc[hh, c])
            o = o * lax.rsqrt(jnp.mean(o * o, axis=-1, keepdims=True) + RMS_EPS)
            hg = pg_sc[rows, sl]
            yh_sc[rows, sl] = (o * nw[:, sl] * (hg * _sigmoid(hg))).astype(BF16)

    def interleave(units, fillers):
        for i, unit in enumerate(units):
            unit()
            for p in fillers[i * len(fillers) // len(units):(i + 1) * len(fillers) // len(units)]:
                p()

    def attn_group(s, g):
        for c in range(n_chunks):
            attn_chunk(s, g, c)

    for p in proj_h:
        p()
    interleave([functools.partial(prep_block, c) for c in range(n_blocks)], proj_q + proj_g)
    fill_bands()
    interleave([functools.partial(attn_group, s, g) for s in range(SB) for g in range(N_KV_HEADS)], proj_z)
    for hh in range(HGRN_HEADS):
        hgrn_outputs(hh)

    for s in range(SB):
        knext = kb_sc[s, TT:TT + WINDOW, :]
        vnext = vb_sc[s, TT:TT + WINDOW, :]
        kb_sc[s, 0:WINDOW, :] = knext
        vb_sc[s, 0:WINDOW, :] = vnext

    pa = _dot(ya_sc[...], wba_ref[...])
    ph = _dot(yh_sc[...], wbh_ref[...])
    merged = (_sigmoid(pz_sc[:, 0:D_MODEL]) * pa
              + _sigmoid(pz_sc[:, D_MODEL:2 * D_MODEL]) * ph)
    mo = _dot(merged.astype(BF16), wout_ref[...])
    for s in range(SB):
        g1 = mod_ref[s, 2:3, :]
        u = ALPHA * x_ref[s] + g1 * mo[s * TT:(s + 1) * TT]
        y_ref[s] = _ln_rows(u) * lng_ref[...] + lnb_ref[...]

    @pl.when(t == last_t)
    def _():
        for s in range(SB):
            for hh in range(HGRN_HEADS):
                sn_ref[s, hh] = s_sc[s, hh]


def _const_spec(shape):
    zeros = (0,) * len(shape)
    return pl.BlockSpec(shape, lambda i, t: zeros, pipeline_mode=pl.Buffered(1))


def _mixer_call(x, mod, ck, cv, st, w_in, wba, wbh, wout, sinks, lbl, nw, lng, lnb, *, SB, TT, pos0):
    nseq, T, _ = x.shape
    TM = SB * TT
    grid = (nseq // SB, T // TT)
    seq_map3 = lambda i, t: (i, 0, 0)
    seq_map4 = lambda i, t: (i, 0, 0, 0)
    kernel = functools.partial(_mixer_kernel, SB=SB, TT=TT, pos0=pos0)
    return pl.pallas_call(
        kernel,
        grid=grid,
        in_specs=[
            pl.BlockSpec((SB, TT, D_MODEL), lambda i, t: (i, t, 0)),
            pl.BlockSpec((SB, 6, D_MODEL), seq_map3),
            pl.BlockSpec((SB, WINDOW, KV_WIDTH), seq_map3),
            pl.BlockSpec((SB, WINDOW, KV_WIDTH), seq_map3),
            pl.BlockSpec((SB, HGRN_HEADS, HGRN_DK, HGRN_DV), seq_map4),
            _const_spec((D_MODEL, IN_WIDTH)),
            _const_spec((ATTN_WIDTH, D_MODEL)),
            _const_spec((HGRN_WIDTH, D_MODEL)),
            _const_spec((D_MODEL, D_MODEL)),
            pl.BlockSpec(memory_space=pltpu.SMEM),
            _const_spec((2, HGRN_WIDTH)),
            _const_spec((1, HGRN_WIDTH)),
            _const_spec((1, D_MODEL)),
            _const_spec((1, D_MODEL)),
        ],
        out_specs=[
            pl.BlockSpec((SB, TT, D_MODEL), lambda i, t: (i, t, 0)),
            pl.BlockSpec((SB, WINDOW, KV_WIDTH), seq_map3),
            pl.BlockSpec((SB, WINDOW, KV_WIDTH), seq_map3),
            pl.BlockSpec((SB, HGRN_HEADS, HGRN_DK, HGRN_DV), seq_map4),
        ],
        out_shape=[
            jax.ShapeDtypeStruct((nseq, T, D_MODEL), F32),
            jax.ShapeDtypeStruct((nseq, WINDOW, KV_WIDTH), F32),
            jax.ShapeDtypeStruct((nseq, WINDOW, KV_WIDTH), F32),
            jax.ShapeDtypeStruct((nseq, HGRN_HEADS, HGRN_DK, HGRN_DV), F32),
        ],
        scratch_shapes=[
            pltpu.VMEM((TM, D_MODEL), BF16),
            pltpu.VMEM((TM, HQ0 - Q0), F32),
            pltpu.VMEM((TM, HG0 - HQ0), F32),
            pltpu.VMEM((TM, GA0 - HG0), F32),
            pltpu.VMEM((TM, IN_WIDTH - GA0), F32),
            pltpu.VMEM((SB, WINDOW + TT + BAND_PAD, KV_WIDTH), F32),
            pltpu.VMEM((SB, WINDOW + TT + BAND_PAD, KV_WIDTH), F32),
            pltpu.VMEM((SB, HGRN_HEADS, HGRN_DK, HGRN_DV), F32),
            pltpu.VMEM((TM, ATTN_WIDTH), BF16),
            pltpu.VMEM((TM, HGRN_WIDTH), BF16),
            pltpu.VMEM((HGRN_HEADS, TM, HGRN_BLOCK), BF16),
            pltpu.VMEM((HGRN_HEADS, TM // HGRN_BLOCK, HGRN_DK, HGRN_DV), BF16),
            pltpu.VMEM((HGRN_HEADS, TM, HGRN_DK), BF16),
            pltpu.VMEM((HGRN_HEADS, TM, HGRN_DV), BF16),
        ],
        compiler_params=pltpu.CompilerParams(
            dimension_semantics=("arbitrary", "arbitrary"), vmem_limit_bytes=VMEM_LIMIT_BYTES),
        name="mixer",
    )(x, mod, ck, cv, st, w_in, wba, wbh, wout, sinks, lbl, nw, lng, lnb)


def _ffn_kernel(x_ref, mod_ref, wup_ref, wdn_ref, lng_ref, lnb_ref, y_ref, h_sc, act_sc, *, SB, TT):
    for s in range(SB):
        sh2 = mod_ref[s, 3:4, :]
        sc2 = mod_ref[s, 4:5, :]
        h = _ln_rows(x_ref[s]) * (1.0 + sc2) + sh2
        h_sc[s * TT:(s + 1) * TT, :] = h.astype(BF16)
    for j in range(D_FF // MXU_WIDTH):
        cols = slice(j * MXU_WIDTH, (j + 1) * MXU_WIDTH)
        gcols = slice(D_FF + j * MXU_WIDTH, D_FF + (j + 1) * MXU_WIDTH)
        u = _dot(h_sc[...], wup_ref[:, cols])
        gt = _dot(h_sc[...], wup_ref[:, gcols])
        act_sc[:, cols] = (gt * _sigmoid(gt) * u).astype(BF16)
    f = _dot(act_sc[...], wdn_ref[...])
    for s in range(SB):
        g2 = mod_ref[s, 5:6, :]
        u = ALPHA * x_ref[s] + g2 * f[s * TT:(s + 1) * TT]
        y_ref[s] = _ln_rows(u) * lng_ref[...] + lnb_ref[...]


def _ffn_call(x, mod, wup, wdn, lng, lnb, *, SB, TT):
    nseq, T, _ = x.shape
    TM = SB * TT
    kernel = functools.partial(_ffn_kernel, SB=SB, TT=TT)
    return pl.pallas_call(
        kernel,
        grid=(nseq // SB, T // TT),
        in_specs=[
            pl.BlockSpec((SB, TT, D_MODEL), lambda i, t: (i, t, 0)),
            pl.BlockSpec((SB, 6, D_MODEL), lambda i, t: (i, 0, 0)),
            _const_spec((D_MODEL, 2 * D_FF)),
            _const_spec((D_FF, D_MODEL)),
            _const_spec((1, D_MODEL)),
            _const_spec((1, D_MODEL)),
        ],
        out_specs=pl.BlockSpec((SB, TT, D_MODEL), lambda i, t: (i, t, 0)),
        out_shape=jax.ShapeDtypeStruct((nseq, T, D_MODEL), F32),
        scratch_shapes=[
            pltpu.VMEM((TM, D_MODEL), BF16),
            pltpu.VMEM((TM, D_FF), BF16),
        ],
        compiler_params=pltpu.CompilerParams(
            dimension_semantics=("arbitrary", "arbitrary"), vmem_limit_bytes=VMEM_LIMIT_BYTES),
        name="ffn",
    )(x, mod, wup, wdn, lng, lnb)


def _ffn_pipe_kernel(xp_ref, xn_ref, modp_ref, modn_ref, wup_ref, wdn_ref, lng_ref, lnb_ref, y_ref,
                     h_sc, act_sc, f_sc):
    n = pl.program_id(0)

    def ln_in(x_ref, mod_ref):
        return _ln_rows(x_ref[...]) * (1.0 + mod_ref[0, 4:5, :]) + mod_ref[0, 3:4, :]

    @pl.when(n == 0)
    def _():
        h_sc[...] = ln_in(xp_ref, modp_ref).astype(BF16)
        f_sc[...] = jnp.zeros(f_sc.shape, F32)

    u = ALPHA * xp_ref[...] + modp_ref[0, 5:6, :] * f_sc[...]
    y = _ln_rows(u) * lng_ref[...] + lnb_ref[...]
    y_ref[...] = y
    h_next = ln_in(xn_ref, modn_ref)

    n_up = D_FF // MXU_WIDTH
    ties = {1 + k: y[:, k * MXU_WIDTH:(k + 1) * MXU_WIDTH] for k in range(D_MODEL // MXU_WIDTH)}
    ties.update({n_up - 5 + k: h_next[:, k * MXU_WIDTH:(k + 1) * MXU_WIDTH]
                 for k in range(D_MODEL // MXU_WIDTH)})
    for j in range(n_up):
        cols = slice(j * MXU_WIDTH, (j + 1) * MXU_WIDTH)
        gcols = slice(D_FF + j * MXU_WIDTH, D_FF + (j + 1) * MXU_WIDTH)
        up = _dot(h_sc[...], wup_ref[:, cols])
        gt = _dot(h_sc[...], wup_ref[:, gcols])
        a = gt * _sigmoid(gt) * up
        if j in ties:
            a = _ordered_after(a, ties[j])
        act_sc[:, cols] = a.astype(BF16)

    h_sc[...] = h_next.astype(BF16)
    f_sc[...] = _dot(act_sc[...], wdn_ref[...])


def _ffn_pipe_call(x, mod, wup, wdn, lng, lnb, *, TT):
    nseq, T, _ = x.shape
    tiles_per_seq = T // TT
    n_tiles = nseq * tiles_per_seq
    prev_tile = lambda n: jnp.maximum(n - 1, 0)
    next_tile = lambda n: jnp.minimum(n + 1, n_tiles - 1)
    const = lambda shape: pl.BlockSpec(shape, lambda n: (0,) * len(shape), pipeline_mode=pl.Buffered(1))
    y = pl.pallas_call(
        _ffn_pipe_kernel,
        grid=(n_tiles + 1,),
        in_specs=[
            pl.BlockSpec((TT, D_MODEL), lambda n: (prev_tile(n), 0)),
            pl.BlockSpec((TT, D_MODEL), lambda n: (next_tile(n), 0)),
            pl.BlockSpec((1, 6, D_MODEL), lambda n: (prev_tile(n) // tiles_per_seq, 0, 0)),
            pl.BlockSpec((1, 6, D_MODEL), lambda n: (next_tile(n) // tiles_per_seq, 0, 0)),
            const((D_MODEL, 2 * D_FF)),
            const((D_FF, D_MODEL)),
            const((1, D_MODEL)),
            const((1, D_MODEL)),
        ],
        out_specs=pl.BlockSpec((TT, D_MODEL), lambda n: (prev_tile(n), 0)),
        out_shape=jax.ShapeDtypeStruct((nseq * T, D_MODEL), F32),
        scratch_shapes=[
            pltpu.VMEM((TT, D_MODEL), BF16),
            pltpu.VMEM((TT, D_FF), BF16),
            pltpu.VMEM((TT, D_MODEL), F32),
        ],
        compiler_params=pltpu.CompilerParams(
            dimension_semantics=("arbitrary",), vmem_limit_bytes=VMEM_LIMIT_BYTES),
        name="ffn_pipelined",
    )(x.reshape(nseq * T, D_MODEL), x.reshape(nseq * T, D_MODEL), mod, mod, wup, wdn, lng, lnb)
    return y.reshape(nseq, T, D_MODEL)


def kernel(x_prompt, x_sample, cache_attn_k, cache_attn_v, state_hgrn, c_prompt, c_sample, w_ada, b_ada, w_in, attn_sinks, hgrn_lb_logits, hgrn_norm_w, w_branch_attn, w_branch_hgrn, w_out, ln_mix_g, ln_mix_b, w_up, w_down, ln_ffn_g, ln_ffn_b):
    assert w_ada.shape[0] == DEPTH and hgrn_lb_logits.shape[0] == DEPTH + 1
    nb = x_prompt.shape[0]
    ns = x_sample.shape[0]
    assert nb + ns <= ADA_ROWS and x_sample.shape[1] == CHUNK

    c_all = jnp.concatenate([c_prompt, c_sample, jnp.zeros((ADA_ROWS - nb - ns, D_MODEL), F32)], axis=0)
    mod = _ada_call(c_all, w_ada[0], b_ada[0][None, :]).reshape(ADA_ROWS, 6, D_MODEL)
    mod_p = mod[:nb]
    mod_s = mod[nb:nb + ns]

    mixer_w = (w_in[0].astype(BF16), w_branch_attn[0].astype(BF16), w_branch_hgrn[0].astype(BF16),
               w_out[0].astype(BF16), attn_sinks[0], hgrn_lb_logits, hgrn_norm_w[0][None, :],
               ln_mix_g[0][None, :], ln_mix_b[0][None, :])
    ffn_w = (w_up[0].astype(BF16), w_down[0].astype(BF16), ln_ffn_g[0][None, :], ln_ffn_b[0][None, :])

    zk = jnp.zeros((nb, WINDOW, KV_WIDTH), F32)
    zs = jnp.zeros((nb, HGRN_HEADS, HGRN_DK, HGRN_DV), F32)
    x1p, kp, vp, sp = _mixer_call(x_prompt, mod_p, zk, zk, zs, *mixer_w, SB=1, TT=512, pos0=0)
    x1s, ks, vs, ss = _mixer_call(
        x_sample, mod_s,
        cache_attn_k[0].reshape(ns, WINDOW, KV_WIDTH), cache_attn_v[0].reshape(ns, WINDOW, KV_WIDTH),
        state_hgrn[0], *mixer_w, SB=4, TT=CHUNK, pos0=PAST_LEN)

    yp = _ffn_pipe_call(x1p, mod_p, *ffn_w, TT=512)
    ys = _ffn_call(x1s, mod_s, *ffn_w, SB=8, TT=CHUNK)

    win = lambda a: a.reshape(1, a.shape[0], WINDOW, N_KV_HEADS, HEAD_DIM)
    return (yp, ys, win(kp), win(vp), sp[None], win(ks), win(vs), ss[None])
```

```python
import functools

import jax
import jax.numpy as jnp
from jax import lax
from jax.experimental import pallas as pl
from jax.experimental.pallas import tpu as pltpu

D_MODEL = 1024
PAST_LEN = 1024
CHUNK = 64
WINDOW = 128
HEAD_DIM = 64
N_Q_HEADS = 8
N_KV_HEADS = 2
Q_GROUP = N_Q_HEADS // N_KV_HEADS
ATTN_WIDTH = N_Q_HEADS * HEAD_DIM
KV_WIDTH = N_KV_HEADS * HEAD_DIM
GROUP_WIDTH = Q_GROUP * HEAD_DIM
HGRN_HEADS = 8
HGRN_DK = 128
HGRN_DV = 128
HGRN_WIDTH = HGRN_HEADS * HGRN_DK
D_FF = 2816
IN_WIDTH = ATTN_WIDTH + 2 * KV_WIDTH + 4 * HGRN_WIDTH + 2 * D_MODEL
DEPTH = 1
ALPHA = (2 * DEPTH) ** 0.25
LN_EPS = 1e-5
RMS_EPS = 1e-6

Q0 = 0
K0 = Q0 + ATTN_WIDTH
V0 = K0 + KV_WIDTH
HQ0 = V0 + KV_WIDTH
HF0 = HQ0 + HGRN_WIDTH
HI0 = HF0 + HGRN_WIDTH
HG0 = HI0 + HGRN_WIDTH
GA0 = HG0 + HGRN_WIDTH
GH0 = GA0 + D_MODEL

LANES = 128
MXU_WIDTH = 256
VMEM_LIMIT_BYTES = 60 * 1024 * 1024

HGRN_BLOCK = 64
HGRN_HALF = HGRN_BLOCK // 2
KEY_WIN = 256
BAND_PAD = KEY_WIN - WINDOW - CHUNK
ADA_ROWS = 40
ADA_BLOCK = 512

BF16 = jnp.bfloat16
F32 = jnp.float32


def _sigmoid(x):
    return 1.0 / (1.0 + jnp.exp(-x))


def _ln_rows(x):
    mu = jnp.mean(x, axis=-1, keepdims=True)
    xc = x - mu
    var = jnp.mean(xc * xc, axis=-1, keepdims=True)
    return xc * lax.rsqrt(var + LN_EPS)


def _dot(a, b):
    return jnp.dot(a, b, preferred_element_type=F32)


def _dot_nt(a, b):
    return lax.dot_general(a, b, (((1,), (1,)), ((), ())), preferred_element_type=F32)


def _cumsum_rows(x):
    n = x.shape[0]
    row = lax.broadcasted_iota(jnp.int32, x.shape, 0)
    s = 1
    while s < n:
        x = x + jnp.where(row >= s, pltpu.roll(x, s, axis=0), 0.0)
        s *= 2
    return x


def _ordered_after(dst, src):
    zero = (lax.bitcast_convert_type(src, jnp.uint32) >> 16) >> 16
    return jnp.where(zero == 0, dst, 0.0)


def _ada_kernel(c_ref, w_ref, b_ref, o_ref):
    c = c_ref[...]
    a = c * _sigmoid(c)
    o_ref[...] = _dot(a.astype(BF16), w_ref[...].astype(BF16)) + b_ref[...]


def _ada_call(c_all, w_ada, b_ada):
    n = w_ada.shape[1]
    return pl.pallas_call(
        _ada_kernel,
        grid=(n // ADA_BLOCK,),
        in_specs=[
            pl.BlockSpec((ADA_ROWS, D_MODEL), lambda j: (0, 0)),
            pl.BlockSpec((D_MODEL, ADA_BLOCK), lambda j: (0, j)),
            pl.BlockSpec((1, ADA_BLOCK), lambda j: (0, j)),
        ],
        out_specs=pl.BlockSpec((ADA_ROWS, ADA_BLOCK), lambda j: (0, j)),
        out_shape=jax.ShapeDtypeStruct((ADA_ROWS, n), F32),
        compiler_params=pltpu.CompilerParams(dimension_semantics=("arbitrary",)),
        name="ada_mod",
    )(c_all, w_ada, b_ada)


def _mixer_kernel(x_ref, mod_ref, ck_ref, cv_ref, st_ref, w_in_ref, wba_ref, wbh_ref, wout_ref,
                  sinks_ref, lbl_ref, nw_ref, lng_ref, lnb_ref,
                  y_ref, kw_ref, vw_ref, sn_ref,
                  h_sc, pq_sc, ph_sc, pg_sc, pz_sc, kb_sc, vb_sc, s_sc, ya_sc, yh_sc,
                  a_sc, s16_sc, qi_sc, v_sc, *, SB, TT, pos0):
    t = pl.program_id(1)
    last_t = pl.num_programs(1) - 1
    n_chunks = TT // CHUNK
    band_rows = WINDOW + TT
    TM = SB * TT
    blocks_per_seq = TT // HGRN_BLOCK
    n_blocks = SB * blocks_per_seq

    @pl.when(t == 0)
    def _():
        for s in range(SB):
            kb_sc[s, 0:WINDOW, :] = ck_ref[s]
            vb_sc[s, 0:WINDOW, :] = cv_ref[s]
            for hh in range(HGRN_HEADS):
                s_sc[s, hh] = st_ref[s, hh]

    for s in range(SB):
        sh1 = mod_ref[s, 0:1, :]
        sc1 = mod_ref[s, 1:2, :]
        h = _ln_rows(x_ref[s]) * (1.0 + sc1) + sh1
        h_sc[s * TT:(s + 1) * TT, :] = h.astype(BF16)

    def project(dst, w_col0, j):
        cols = slice(j * MXU_WIDTH, (j + 1) * MXU_WIDTH)
        wcols = slice(w_col0 + j * MXU_WIDTH, w_col0 + (j + 1) * MXU_WIDTH)
        dst[:, cols] = _dot(h_sc[...], w_in_ref[:, wcols])

    proj_q = [functools.partial(project, pq_sc, Q0, j) for j in range((HQ0 - Q0) // MXU_WIDTH)]
    proj_h = [functools.partial(project, ph_sc, HQ0, j) for j in range((HG0 - HQ0) // MXU_WIDTH)]
    proj_g = [functools.partial(project, pg_sc, HG0, j) for j in range((GA0 - HG0) // MXU_WIDTH)]
    proj_z = [functools.partial(project, pz_sc, GA0, j) for j in range((IN_WIDTH - GA0) // MXU_WIDTH)]

    l0 = lbl_ref[0:1, :]
    l1 = lbl_ref[1:2, :]
    lmax = jnp.maximum(l0, l1)
    e0 = jnp.exp(l0 - lmax)
    lb = e0 / (e0 + jnp.exp(l1 - lmax))
    oml = 1.0 - lb
    in_h1 = lax.broadcasted_iota(jnp.int32, (HGRN_BLOCK, 1), 0) < HGRN_HALF
    causal = (lax.broadcasted_iota(jnp.int32, (HGRN_BLOCK, HGRN_BLOCK), 0)
              >= lax.broadcasted_iota(jnp.int32, (HGRN_BLOCK, HGRN_BLOCK), 1))
    m1 = HGRN_HALF // 2 - 1
    m2 = HGRN_HALF + m1

    def prep_block(c):
        rows = slice(c * HGRN_BLOCK, (c + 1) * HGRN_BLOCK)
        sg = _sigmoid(ph_sc[rows, HGRN_WIDTH:2 * HGRN_WIDTH])
        log_f = jnp.log(lb + oml * sg)
        k_in = oml * (1.0 - sg)
        b = _cumsum_rows(log_f)
        r1 = b[m1:m1 + 1, :]
        r2 = b[m2:m2 + 1, :]
        b_last = b[HGRN_BLOCK - 1:HGRN_BLOCK, :]
        rh = jnp.where(in_h1, r1, r2)
        hq = ph_sc[rows, 0:HGRN_WIDTH]
        qe = hq * _sigmoid(hq) * jnp.exp(b - rh)
        ke = k_in * jnp.exp(rh - b)
        qi = qe * jnp.where(in_h1, jnp.exp(r1), jnp.exp(r2))
        kd = ke * jnp.where(in_h1, jnp.exp(b_last - r1), jnp.exp(b_last - r2))
        la = jnp.where(in_h1, qe, qe * jnp.exp(r2 - r1))
        lb2 = jnp.where(in_h1, 0.0, qe)
        ra = jnp.where(in_h1, ke, 0.0)
        rb = jnp.where(in_h1, 0.0, ke)
        v = ph_sc[rows, 2 * HGRN_WIDTH:3 * HGRN_WIDTH]
        dec_col = jnp.exp(b_last).T
        la16, lb16, ra16, rb16, qi16, kdt16, v16 = (
            a.astype(BF16) for a in (la, lb2, ra, rb, qi, kd.T, v))
        s = c // blocks_per_seq
        for hh in range(HGRN_HEADS):
            sl = slice(hh * HGRN_DK, (hh + 1) * HGRN_DK)
            lhs = jnp.concatenate([la16[:, sl], lb16[:, sl]], axis=1)
            rhs = jnp.concatenate([ra16[:, sl], rb16[:, sl]], axis=1)
            a_sc[hh, rows, :] = jnp.where(causal, _dot_nt(lhs, rhs), 0.0).astype(BF16)
            qi_sc[hh, rows, :] = qi16[:, sl]
            v_sc[hh, rows, :] = v16[:, sl]
            st = s_sc[s, hh]
            s16_sc[hh, c] = st.astype(BF16)
            s_sc[s, hh] = dec_col[sl, :] * st + _dot(kdt16[sl, :], v16[:, sl])

    lane = lax.broadcasted_iota(jnp.int32, (1, LANES), 1)
    row =lax.broadcasted_iota(jnp.int32, (Q_GROUP * CHUNK, 1), 0)
    head_of_row = row >> 6
    q_of_row = row & (CHUNK - 1)
    key = lax.broadcasted_iota(jnp.int32, (1, KEY_WIN), 1)
    dist = jnp.abs(q_of_row + WINDOW - key).astype(F32)
    first_half = lane < HEAD_DIM

    @functools.cache
    def attn_operands(s, g):
        kband = kb_sc[s]
        vband = vb_sc[s]
        kroll = pltpu.roll(kband, HEAD_DIM, axis=1)
        vroll = pltpu.roll(vband, HEAD_DIM, axis=1)
        kk = jnp.where(first_half, kband if g == 0 else kroll, 0.0).astype(BF16)
        v1 = jnp.where(first_half, vband, vroll) if g == 0 else jnp.where(first_half, vroll, vband)
        vv = v1.astype(BF16)
        slope = jnp.zeros((Q_GROUP * CHUNK, 1), F32)
        sink = jnp.zeros((Q_GROUP * CHUNK, 1), F32)
        for hq in range(Q_GROUP):
            slope = jnp.where(head_of_row == hq, 2.0 ** -(g * Q_GROUP + hq + 1), slope)
            sink = jnp.where(head_of_row == hq, sinks_ref[g * Q_GROUP + hq], sink)
        return kk, vv, slope * dist, sink

    def attn_chunk(s, g, c):
        kk, vv, bias, sink = attn_operands(s, g)
        rq = s * TT + c * CHUNK
        qg = pq_sc[rq:rq + CHUNK, g * GROUP_WIDTH:(g + 1) * GROUP_WIDTH] * (HEAD_DIM ** -0.5)
        parts = []
        for j in range(GROUP_WIDTH // LANES):
            pair = qg[:, j * LANES:(j + 1) * LANES]
            parts.append(jnp.where(first_half, pair, 0.0))
            parts.append(jnp.where(first_half, pltpu.roll(pair, HEAD_DIM, axis=1), 0.0))
        lhs = jnp.concatenate(parts, axis=0).astype(BF16)
        kwin = kk[c * CHUNK:c * CHUNK + KEY_WIN]
        vwin = vv[c * CHUNK:c * CHUNK + KEY_WIN]
        sc = _dot_nt(lhs, kwin) - bias
        ok = key < WINDOW + CHUNK
        if pos0 < WINDOW:
            ok = ok & (key >= WINDOW - pos0 - c * CHUNK - t * TT)
        sc = jnp.where(ok, sc, -jnp.inf)
        m = jnp.maximum(jnp.max(sc, axis=-1, keepdims=True), sink)
        e = jnp.exp(sc - m)
        den = jnp.sum(e, axis=-1, keepdims=True) + jnp.exp(sink - m)
        r = _dot(e.astype(BF16), vwin) / den
        out = jnp.concatenate(
            [jnp.where(first_half, r[2 * j * CHUNK:(2 * j + 1) * CHUNK], r[(2 * j + 1) * CHUNK:(2 * j + 2) * CHUNK])
             for j in range(GROUP_WIDTH // LANES)], axis=1)
        ya_sc[rq:rq + CHUNK, g * GROUP_WIDTH:(g + 1) * GROUP_WIDTH] = out.astype(BF16)

    def fill_bands():
        for s in range(SB):
            r0 = s * TT
            kb_sc[s, WINDOW:band_rows, :] = pq_sc[r0:r0 + TT, K0:K0 + KV_WIDTH]
            vb_sc[s, WINDOW:band_rows, :] = pq_sc[r0:r0 + TT, V0:V0 + KV_WIDTH]
            kb_sc[s, band_rows:, :] = jnp.zeros((BAND_PAD, KV_WIDTH), F32)
            vb_sc[s, band_rows:, :] = jnp.zeros((BAND_PAD, KV_WIDTH), F32)
            kw_ref[s] = kb_sc[s, TT:TT + WINDOW, :]
            vw_ref[s] = vb_sc[s, TT:TT + WINDOW, :]

    nw = nw_ref[...]

    def hgrn_outputs(hh):
        sl = slice(hh * HGRN_DV, (hh + 1) * HGRN_DV)
        for c in range(n_blocks):
            rows = slice(c * HGRN_BLOCK, (c + 1) * HGRN_BLOCK)
            o = _dot(a_sc[hh, rows, :], v_sc[hh, rows, :]) + _dot(qi_sc[hh, rows, :], s16_sc[hh, c])
            o = o * lax.rsqrt(jnp.mean(o * o, axis=-1, keepdims=True) + RMS_EPS)
            hg = pg_sc[rows, sl]
            yh_sc[rows, sl] = (o * nw[:, sl] * (hg * _sigmoid(hg))).astype(BF16)

    def interleave(units, fillers):
        for i, unit in enumerate(units):
            unit()
            for p in fillers[i * len(fillers) // len(units):(i + 1) * len(fillers) // len(units)]:
                p()

    def attn_group(s, g):
        for c in range(n_chunks):
            attn_chunk(s, g, c)

    for p in proj_h:
        p()
    interleave([functools.partial(prep_block, c) for c in range(n_blocks)], proj_q + proj_g)
    fill_bands()
    interleave([functools.partial(attn_group, s, g) for s in range(SB) for g in range(N_KV_HEADS)], proj_z)
    for hh in range(HGRN_HEADS):
        hgrn_outputs(hh)

    for s in range(SB):
        knext = kb_sc[s, TT:TT + WINDOW, :]
        vnext = vb_sc[s, TT:TT + WINDOW, :]
        kb_sc[s, 0:WINDOW, :] = knext
        vb_sc[s, 0:WINDOW, :] = vnext

    pa = _dot(ya_sc[...], wba_ref[...])
    ph = _dot(yh_sc[...], wbh_ref[...])
    merged = (_sigmoid(pz_sc[:, 0:D_MODEL]) * pa
              + _sigmoid(pz_sc[:, D_MODEL:2 * D_MODEL]) * ph)
    mo = _dot(merged.astype(BF16), wout_ref[...])
    for s in range(SB):
        g1 = mod_ref[s, 2:3, :]
        u = ALPHA * x_ref[s] + g1 * mo[s * TT:(s + 1) * TT]
        y_ref[s] = _ln_rows(u) * lng_ref[...] + lnb_ref[...]

    @pl.when(t == last_t)
    def _():
        for s in range(SB):
            for hh in range(HGRN_HEADS):
                sn_ref[s, hh] = s_sc[s, hh]


def _const_spec(shape):
    zeros = (0,) * len(shape)
    return pl.BlockSpec(shape, lambda i, t: zeros, pipeline_mode=pl.Buffered(1))


def _mixer_call(x, mod, ck, cv, st, w_in, wba, wbh, wout, sinks, lbl, nw, lng, lnb, *, SB, TT, pos0):
    nseq, T, _ = x.shape
    TM = SB * TT
    grid = (nseq // SB, T // TT)
    seq_map3 = lambda i, t: (i, 0, 0)
    seq_map4 = lambda i, t: (i, 0, 0, 0)
    kernel = functools.partial(_mixer_kernel, SB=SB, TT=TT, pos0=pos0)
    return pl.pallas_call(
        kernel,
        grid=grid,
        in_specs=[
            pl.BlockSpec((SB, TT, D_MODEL), lambda i, t: (i, t, 0)),
            pl.BlockSpec((SB, 6, D_MODEL), seq_map3),
            pl.BlockSpec((SB, WINDOW, KV_WIDTH), seq_map3),
            pl.BlockSpec((SB, WINDOW, KV_WIDTH), seq_map3),
            pl.BlockSpec((SB, HGRN_HEADS, HGRN_DK, HGRN_DV), seq_map4),
            _const_spec((D_MODEL, IN_WIDTH)),
            _const_spec((ATTN_WIDTH, D_MODEL)),
            _const_spec((HGRN_WIDTH, D_MODEL)),
            _const_spec((D_MODEL, D_MODEL)),
            pl.BlockSpec(memory_space=pltpu.SMEM),
            _const_spec((2, HGRN_WIDTH)),
            _const_spec((1, HGRN_WIDTH)),
            _const_spec((1, D_MODEL)),
            _const_spec((1, D_MODEL)),
        ],
        out_specs=[
            pl.BlockSpec((SB, TT, D_MODEL), lambda i, t: (i, t, 0)),
            pl.BlockSpec((SB, WINDOW, KV_WIDTH), seq_map3),
            pl.BlockSpec((SB, WINDOW, KV_WIDTH), seq_map3),
            pl.BlockSpec((SB, HGRN_HEADS, HGRN_DK, HGRN_DV), seq_map4),
        ],
        out_shape=[
            jax.ShapeDtypeStruct((nseq, T, D_MODEL), F32),
            jax.ShapeDtypeStruct((nseq, WINDOW, KV_WIDTH), F32),
            jax.ShapeDtypeStruct((nseq, WINDOW, KV_WIDTH), F32),
            jax.ShapeDtypeStruct((nseq, HGRN_HEADS, HGRN_DK, HGRN_DV), F32),
        ],
        scratch_shapes=[
            pltpu.VMEM((TM, D_MODEL), BF16),
            pltpu.VMEM((TM, HQ0 - Q0), F32),
            pltpu.VMEM((TM, HG0 - HQ0), F32),
            pltpu.VMEM((TM, GA0 - HG0), F32),
            pltpu.VMEM((TM, IN_WIDTH - GA0), F32),
            pltpu.VMEM((SB, WINDOW + TT + BAND_PAD, KV_WIDTH), F32),
            pltpu.VMEM((SB, WINDOW + TT + BAND_PAD, KV_WIDTH), F32),
            pltpu.VMEM((SB, HGRN_HEADS, HGRN_DK, HGRN_DV), F32),
            pltpu.VMEM((TM, ATTN_WIDTH), BF16),
            pltpu.VMEM((TM, HGRN_WIDTH), BF16),
            pltpu.VMEM((HGRN_HEADS, TM, HGRN_BLOCK), BF16),
            pltpu.VMEM((HGRN_HEADS, TM // HGRN_BLOCK, HGRN_DK, HGRN_DV), BF16),
            pltpu.VMEM((HGRN_HEADS, TM, HGRN_DK), BF16),
            pltpu.VMEM((HGRN_HEADS, TM, HGRN_DV), BF16),
        ],
        compiler_params=pltpu.CompilerParams(
            dimension_semantics=("arbitrary", "arbitrary"), vmem_limit_bytes=VMEM_LIMIT_BYTES),
        name="mixer",
    )(x, mod, ck, cv, st, w_in, wba, wbh, wout, sinks, lbl, nw, lng, lnb)


def _ffn_kernel(x_ref, mod_ref, wup_ref, wdn_ref, lng_ref, lnb_ref, y_ref, h_sc, act_sc, *, SB, TT):
    for s in range(SB):
        sh2 = mod_ref[s, 3:4, :]
        sc2 = mod_ref[s, 4:5, :]
        h = _ln_rows(x_ref[s]) * (1.0 + sc2) + sh2
        h_sc[s * TT:(s + 1) * TT, :] = h.astype(BF16)
    for j in range(D_FF // MXU_WIDTH):
        cols = slice(j * MXU_WIDTH, (j + 1) * MXU_WIDTH)
        gcols = slice(D_FF + j * MXU_WIDTH, D_FF + (j + 1) * MXU_WIDTH)
        u = _dot(h_sc[...], wup_ref[:, cols])
        gt = _dot(h_sc[...], wup_ref[:, gcols])
        act_sc[:, cols] = (gt * _sigmoid(gt) * u).astype(BF16)
    f = _dot(act_sc[...], wdn_ref[...])
    for s in range(SB):
        g2 = mod_ref[s, 5:6, :]
        u = ALPHA * x_ref[s] + g2 * f[s * TT:(s + 1) * TT]
        y_ref[s] = _ln_rows(u) * lng_ref[...] + lnb_ref[...]


def _ffn_call(x, mod, wup, wdn, lng, lnb, *, SB, TT):
    nseq, T, _ = x.shape
    TM = SB * TT
    kernel = functools.partial(_ffn_kernel, SB=SB, TT=TT)
    return pl.pallas_call(
        kernel,
        grid=(nseq // SB, T // TT),
        in_specs=[
            pl.BlockSpec((SB, TT, D_MODEL), lambda i, t: (i, t, 0)),
            pl.BlockSpec((SB, 6, D_MODEL), lambda i, t: (i, 0, 0)),
            _const_spec((D_MODEL, 2 * D_FF)),
            _const_spec((D_FF, D_MODEL)),
            _const_spec((1, D_MODEL)),
            _const_spec((1, D_MODEL)),
        ],
        out_specs=pl.BlockSpec((SB, TT, D_MODEL), lambda i, t: (i, t, 0)),
        out_shape=jax.ShapeDtypeStruct((nseq, T, D_MODEL), F32),
        scratch_shapes=[
            pltpu.VMEM((TM, D_MODEL), BF16),
            pltpu.VMEM((TM, D_FF), BF16),
        ],
        compiler_params=pltpu.CompilerParams(
            dimension_semantics=("arbitrary", "arbitrary"), vmem_limit_bytes=VMEM_LIMIT_BYTES),
        name="ffn",
    )(x, mod, wup, wdn, lng, lnb)


def _ffn_pipe_kernel(xp_ref, xn_ref, modp_ref, modn_ref, wup_ref, wdn_ref, lng_ref, lnb_ref, y_ref,
                     h_sc, act_sc, f_sc):
    n = pl.program_id(0)

    def ln_in(x_ref, mod_ref):
        return _ln_rows(x_ref[...]) * (1.0 + mod_ref[0, 4:5, :]) + mod_ref[0, 3:4, :]

    @pl.when(n == 0)
    def _():
        h_sc[...] = ln_in(xp_ref, modp_ref).astype(BF16)
        f_sc[...] = jnp.zeros(f_sc.shape, F32)

    u = ALPHA * xp_ref[...] + modp_ref[0, 5:6, :] * f_sc[...]
    y = _ln_rows(u) * lng_ref[...] + lnb_ref[...]
    y_ref[...] = y
    h_next = ln_in(xn_ref, modn_ref)

    n_up = D_FF // MXU_WIDTH
    ties = {1 + k: y[:, k * MXU_WIDTH:(k + 1) * MXU_WIDTH] for k in range(D_MODEL // MXU_WIDTH)}
    ties.update({n_up - 5 + k: h_next[:, k * MXU_WIDTH:(k + 1) * MXU_WIDTH]
                 for k in range(D_MODEL // MXU_WIDTH)})
    for j in range(n_up):
        cols = slice(j * MXU_WIDTH, (j + 1) * MXU_WIDTH)
        gcols = slice(D_FF + j * MXU_WIDTH, D_FF + (j + 1) * MXU_WIDTH)
        up = _dot(h_sc[...], wup_ref[:, cols])
        gt = _dot(h_sc[...], wup_ref[:, gcols])
        a = gt * _sigmoid(gt) * up
        if j in ties:
            a = _ordered_after(a, ties[j])
        act_sc[:, cols] = a.astype(BF16)

    h_sc[...] = h_next.astype(BF16)
    f_sc[...] = _dot(act_sc[...], wdn_ref[...])


def _ffn_pipe_call(x, mod, wup, wdn, lng, lnb, *, TT):
    nseq, T, _ = x.shape
    tiles_per_seq = T // TT
    n_tiles = nseq * tiles_per_seq
    prev_tile = lambda n: jnp.maximum(n - 1, 0)
    next_tile = lambda n: jnp.minimum(n + 1, n_tiles - 1)
    const = lambda shape: pl.BlockSpec(shape, lambda n: (0,) * len(shape), pipeline_mode=pl.Buffered(1))
    y = pl.pallas_call(
        _ffn_pipe_kernel,
        grid=(n_tiles + 1,),
        in_specs=[
            pl.BlockSpec((TT, D_MODEL), lambda n: (prev_tile(n), 0)),
            pl.BlockSpec((TT, D_MODEL), lambda n: (next_tile(n), 0)),
            pl.BlockSpec((1, 6, D_MODEL), lambda n: (prev_tile(n) // tiles_per_seq, 0, 0)),
            pl.BlockSpec((1, 6, D_MODEL), lambda n: (next_tile(n) // tiles_per_seq, 0, 0)),
            const((D_MODEL, 2 * D_FF)),
            const((D_FF, D_MODEL)),
            const((1, D_MODEL)),
            const((1, D_MODEL)),
        ],
        out_specs=pl.BlockSpec((TT, D_MODEL), lambda n: (prev_tile(n), 0)),
        out_shape=jax.ShapeDtypeStruct((nseq * T, D_MODEL), F32),
        scratch_shapes=[
            pltpu.VMEM((TT, D_MODEL), BF16),
            pltpu.VMEM((TT, D_FF), BF16),
            pltpu.VMEM((TT, D_MODEL), F32),
        ],
        compiler_params=pltpu.CompilerParams(
            dimension_semantics=("arbitrary",), vmem_limit_bytes=VMEM_LIMIT_BYTES),
        name="ffn_pipelined",
    )(x.reshape(nseq * T, D_MODEL), x.reshape(nseq * T, D_MODEL), mod, mod, wup, wdn, lng, lnb)
    return y.reshape(nseq, T, D_MODEL)


def kernel(x_prompt, x_sample, cache_attn_k, cache_attn_v, state_hgrn, c_prompt, c_sample, w_ada, b_ada, w_in, attn_sinks, hgrn_lb_logits, hgrn_norm_w, w_branch_attn, w_branch_hgrn, w_out, ln_mix_g, ln_mix_b, w_up, w_down, ln_ffn_g, ln_ffn_b):
    assert w_ada.shape[0] == DEPTH and hgrn_lb_logits.shape[0] == DEPTH + 1
    nb = x_prompt.shape[0]
    ns = x_sample.shape[0]
    assert nb + ns <= ADA_ROWS and x_sample.shape[1] == CHUNK

    c_all = jnp.concatenate([c_prompt, c_sample, jnp.zeros((ADA_ROWS - nb - ns, D_MODEL), F32)], axis=0)
    mod = _ada_call(c_all, w_ada[0], b_ada[0][None, :]).reshape(ADA_ROWS, 6, D_MODEL)
    mod_p = mod[:nb]
    mod_s = mod[nb:nb + ns]

    mixer_w = (w_in[0].astype(BF16), w_branch_attn[0].astype(BF16), w_branch_hgrn[0].astype(BF16),
               w_out[0].astype(BF16), attn_sinks[0], hgrn_lb_logits, hgrn_norm_w[0][None, :],
               ln_mix_g[0][None, :], ln_mix_b[0][None, :])
    ffn_w = (w_up[0].astype(BF16), w_down[0].astype(BF16), ln_ffn_g[0][None, :], ln_ffn_b[0][None, :])

    zk = jnp.zeros((nb, WINDOW, KV_WIDTH), F32)
    zs = jnp.zeros((nb, HGRN_HEADS, HGRN_DK, HGRN_DV), F32)
    x1p, kp, vp, sp = _mixer_call(x_prompt, mod_p, zk, zk, zs, *mixer_w, SB=1, TT=512, pos0=0)
    x1s, ks, vs, ss = _mixer_call(
        x_sample, mod_s,
        cache_attn_k[0].reshape(ns, WINDOW, KV_WIDTH), cache_attn_v[0].reshape(ns, WINDOW, KV_WIDTH),
        state_hgrn[0], *mixer_w, SB=4, TT=CHUNK, pos0=PAST_LEN)

    yp = _ffn_pipe_call(x1p, mod_p, *ffn_w, TT=512)
    ys = _ffn_call(x1s, mod_s, *ffn_w, SB=8, TT=CHUNK)

    win = lambda a: a.reshape(1, a.shape[0], WINDOW, N_KV_HEADS, HEAD_DIM)
    return (yp, ys, win(kp), win(vp), sp[None], win(ks), win(vs), ss[None])
```

```python
import functools

import jax
import jax.numpy as jnp
from jax import lax
from jax.experimental import pallas as pl
from jax.experimental.pallas import tpu as pltpu

D_MODEL = 1024
PAST_LEN = 1024
CHUNK = 64
WINDOW = 128
HEAD_DIM = 64
N_Q_HEADS = 8
N_KV_HEADS = 2
Q_GROUP = N_Q_HEADS // N_KV_HEADS
ATTN_WIDTH = N_Q_HEADS * HEAD_DIM
KV_WIDTH = N_KV_HEADS * HEAD_DIM
GROUP_WIDTH = Q_GROUP * HEAD_DIM
HGRN_HEADS = 8
HGRN_DK = 128
HGRN_DV = 128
HGRN_WIDTH = HGRN_HEADS * HGRN_DK
D_FF = 2816
IN_WIDTH = ATTN_WIDTH + 2 * KV_WIDTH + 4 * HGRN_WIDTH + 2 * D_MODEL
DEPTH = 1
ALPHA = (2 * DEPTH) ** 0.25
LN_EPS = 1e-5
RMS_EPS = 1e-6

Q0 = 0
K0 = Q0 + ATTN_WIDTH
V0 = K0 + KV_WIDTH
HQ0 = V0 + KV_WIDTH
HF0 = HQ0 + HGRN_WIDTH
HI0 = HF0 + HGRN_WIDTH
HG0 = HI0 + HGRN_WIDTH
GA0 = HG0 + HGRN_WIDTH
GH0 = GA0 + D_MODEL

LANES = 128
MXU_WIDTH = 256
VMEM_LIMIT_BYTES = 60 * 1024 * 1024

PROMPT_TILE = 512
MIXER_SAMPLE_SEQS = 4
FFN_SAMPLE_SEQS = 8

HGRN_BLOCK = 64
HGRN_HALF = HGRN_BLOCK // 2
KEY_WIN = 256
BAND_PAD = KEY_WIN - WINDOW - CHUNK
ADA_ROWS = 40
ADA_BLOCK = 1024

BF16 = jnp.bfloat16
F32 = jnp.float32


def _sigmoid(x):
    return 1.0 / (1.0 + jnp.exp(-x))


def _ln_rows(x):
    mu = jnp.mean(x, axis=-1, keepdims=True)
    xc = x - mu
    var = jnp.mean(xc * xc, axis=-1, keepdims=True)
    return xc * lax.rsqrt(var + LN_EPS)


def _dot(a, b):
    return jnp.dot(a, b, preferred_element_type=F32)


def _dot_nt(a, b):
    return lax.dot_general(a, b, (((1,), (1,)), ((), ())), preferred_element_type=F32)


def _cumsum_rows(x):
    n = x.shape[0]
    row = lax.broadcasted_iota(jnp.int32, x.shape, 0)
    s = 1
    while s < n:
        x = x + jnp.where(row >= s, pltpu.roll(x, s, axis=0), 0.0)
        s *= 2
    return x


def _ada_kernel(c_ref, w_ref, b_ref, o_ref):
    c = c_ref[...]
    a = c * _sigmoid(c)
    o_ref[...] = _dot(a.astype(BF16), w_ref[...].astype(BF16)) + b_ref[...]


def _ada_call(c_all, w_ada, b_ada):
    n = w_ada.shape[1]
    return pl.pallas_call(
        _ada_kernel,
        grid=(n // ADA_BLOCK,),
        in_specs=[
            pl.BlockSpec((ADA_ROWS, D_MODEL), lambda j: (0, 0)),
            pl.BlockSpec((D_MODEL, ADA_BLOCK), lambda j: (0, j)),
            pl.BlockSpec((1, ADA_BLOCK), lambda j: (0, j)),
        ],
        out_specs=pl.BlockSpec((ADA_ROWS, ADA_BLOCK), lambda j: (0, j)),
        out_shape=jax.ShapeDtypeStruct((ADA_ROWS, n), F32),
        compiler_params=pltpu.CompilerParams(dimension_semantics=("arbitrary",)),
        name="ada_mod",
    )(c_all, w_ada, b_ada)


def _mixer_kernel(x_ref, mod_ref, ck_ref, cv_ref, st_ref, w_in_ref, wba_ref, wbh_ref, wout_ref,
                  sinks_ref, lbl_ref, nw_ref, lng_ref, lnb_ref,
                  y_ref, kw_ref, vw_ref, sn_ref,
                  h_sc, pq_sc, ph_sc, pi_sc, pg_sc, pz_sc, kb_sc, vb_sc, s_sc, ya_sc, yh_sc,
                  a_sc, s16_sc, qi_sc, v_sc, *, SB, TT, pos0):
    t = pl.program_id(1)
    last_t = pl.num_programs(1) - 1
    n_chunks = TT // CHUNK
    band_rows = WINDOW + TT
    TM = SB * TT
    blocks_per_seq = TT // HGRN_BLOCK
    n_blocks = SB * blocks_per_seq

    @pl.when(t == 0)
    def _():
        for s in range(SB):
            kb_sc[s, 0:WINDOW, :] = ck_ref[s]
            vb_sc[s, 0:WINDOW, :] = cv_ref[s]
            for hh in range(HGRN_HEADS):
                s_sc[s, hh] = st_ref[s, hh]

    for s in range(SB):
        sh1 = mod_ref[s, 0:1, :]
        sc1 = mod_ref[s, 1:2, :]
        h = _ln_rows(x_ref[s]) * (1.0 + sc1) + sh1
        h_sc[s * TT:(s + 1) * TT, :] = h.astype(BF16)

    def project(dst, w_col0, j):
        cols = slice(j * MXU_WIDTH, (j + 1) * MXU_WIDTH)
        wcols = slice(w_col0 + j * MXU_WIDTH, w_col0 + (j + 1) * MXU_WIDTH)
        dst[:, cols] = _dot(h_sc[...], w_in_ref[:, wcols]).astype(dst.dtype)

    proj_q = [functools.partial(project, pq_sc, Q0, j) for j in range((HQ0 - Q0) // MXU_WIDTH)]
    proj_h = ([functools.partial(project, ph_sc, HQ0, j) for j in range((HI0 - HQ0) // MXU_WIDTH)]
              + [functools.partial(project, pi_sc, HI0, j) for j in range((HG0 - HI0) // MXU_WIDTH)])
    proj_g = [functools.partial(project, pg_sc, HG0, j) for j in range((GA0 - HG0) // MXU_WIDTH)]
    proj_z = [functools.partial(project, pz_sc, GA0, j) for j in range((IN_WIDTH - GA0) // MXU_WIDTH)]

    l0 = lbl_ref[0:1, :]
    l1 = lbl_ref[1:2, :]
    lmax = jnp.maximum(l0, l1)
    e0 = jnp.exp(l0 - lmax)
    lb = e0 / (e0 + jnp.exp(l1 - lmax))
    oml = 1.0 - lb
    in_h1 = lax.broadcasted_iota(jnp.int32, (HGRN_BLOCK, 1), 0) < HGRN_HALF
    causal = (lax.broadcasted_iota(jnp.int32, (HGRN_BLOCK, HGRN_BLOCK), 0)
              >= lax.broadcasted_iota(jnp.int32, (HGRN_BLOCK, HGRN_BLOCK), 1))
    m1 = HGRN_HALF // 2 - 1
    m2 = HGRN_HALF + m1

    def prep_block(c):
        rows = slice(c * HGRN_BLOCK, (c + 1) * HGRN_BLOCK)
        sg = _sigmoid(ph_sc[rows, HGRN_WIDTH:2 * HGRN_WIDTH])
        log_f = jnp.log(lb + oml * sg)
        k_in = oml * (1.0 - sg)
        b = _cumsum_rows(log_f)
        r1 = b[m1:m1 + 1, :]
        r2 = b[m2:m2 + 1, :]
        b_last = b[HGRN_BLOCK - 1:HGRN_BLOCK, :]
        rh = jnp.where(in_h1, r1, r2)
        hq = ph_sc[rows, 0:HGRN_WIDTH]
        qe = hq * _sigmoid(hq) * jnp.exp(b - rh)
        ke = k_in * jnp.exp(rh - b)
        qi = qe * jnp.where(in_h1, jnp.exp(r1), jnp.exp(r2))
        kd = ke * jnp.where(in_h1, jnp.exp(b_last - r1), jnp.exp(b_last - r2))
        la = jnp.where(in_h1, qe, qe * jnp.exp(r2 - r1))
        lb2 = jnp.where(in_h1, 0.0, qe)
        ra = jnp.where(in_h1, ke, 0.0)
        rb = jnp.where(in_h1, 0.0, ke)
        dec_col = jnp.exp(b_last).T
        la16, lb16, ra16, rb16, qi16, kdt16 = (a.astype(BF16) for a in (la, lb2, ra, rb, qi, kd.T))
        v16 = pi_sc[rows, :]
        s = c // blocks_per_seq
        for hh in range(HGRN_HEADS):
            sl = slice(hh * HGRN_DK, (hh + 1) * HGRN_DK)
            lhs = jnp.concatenate([la16[:, sl], lb16[:, sl]], axis=1)
            rhs = jnp.concatenate([ra16[:, sl], rb16[:, sl]], axis=1)
            a_sc[hh, rows, :] = jnp.where(causal, _dot_nt(lhs, rhs), 0.0).astype(BF16)
            qi_sc[hh, rows, :] = qi16[:, sl]
            v_sc[hh, rows, :] = v16[:, sl]
            st = s_sc[s, hh]
            s16_sc[hh, c] = st.astype(BF16)
            s_sc[s, hh] = dec_col[sl, :] * st + _dot(kdt16[sl, :], v16[:, sl])

    lane = lax.broadcasted_iota(jnp.int32, (1, LANES), 1)
    row = lax.broadcasted_iota(jnp.int32, (Q_GROUP * CHUNK, 1), 0)
    head_of_row = row >> 6
    q_of_row = row & (CHUNK - 1)
    key = lax.broadcasted_iota(jnp.int32, (1, KEY_WIN), 1)
    dist = jnp.abs(q_of_row + WINDOW - key).astype(F32)
    first_half = lane < HEAD_DIM

    @functools.cache
    def attn_operands(s, g):
        kband = kb_sc[s]
        vband = vb_sc[s]
        kroll = pltpu.roll(kband, HEAD_DIM, axis=1)
        vroll = pltpu.roll(vband, HEAD_DIM, axis=1)
        kk = jnp.where(first_half, kband if g == 0 else kroll, 0.0).astype(BF16)
        v1 = jnp.where(first_half, vband, vroll) if g == 0 else jnp.where(first_half, vroll, vband)
        vv = v1.astype(BF16)
        slope = jnp.zeros((Q_GROUP * CHUNK, 1), F32)
        sink = jnp.zeros((Q_GROUP * CHUNK, 1), F32)
        for hq in range(Q_GROUP):
            slope = jnp.where(head_of_row == hq, 2.0 ** -(g * Q_GROUP + hq + 1), slope)
            sink = jnp.where(head_of_row == hq, sinks_ref[g * Q_GROUP + hq], sink)
        return kk, vv, slope * dist, sink

    def attn_chunk(s, g, c):
        kk, vv, bias, sink = attn_operands(s, g)
        rq = s * TT + c * CHUNK
        qg = pq_sc[rq:rq + CHUNK, g * GROUP_WIDTH:(g + 1) * GROUP_WIDTH] * (HEAD_DIM ** -0.5)
        parts = []
        for j in range(GROUP_WIDTH // LANES):
            pair = qg[:, j * LANES:(j + 1) * LANES]
            parts.append(jnp.where(first_half, pair, 0.0))
            parts.append(jnp.where(first_half, pltpu.roll(pair, HEAD_DIM, axis=1), 0.0))
        lhs = jnp.concatenate(parts, axis=0).astype(BF16)
        kwin = kk[c * CHUNK:c * CHUNK + KEY_WIN]
        vwin = vv[c * CHUNK:c * CHUNK + KEY_WIN]
        sc = _dot_nt(lhs, kwin) - bias
        ok = key < WINDOW + CHUNK
        if pos0 < WINDOW:
            ok = ok & (key >= WINDOW - pos0 - c * CHUNK - t * TT)
        sc = jnp.where(ok, sc, -jnp.inf)
        m = jnp.maximum(jnp.max(sc, axis=-1, keepdims=True), sink)
        e = jnp.exp(sc - m)
        den = jnp.sum(e, axis=-1, keepdims=True) + jnp.exp(sink - m)
        r = _dot(e.astype(BF16), vwin) / den
        out = jnp.concatenate(
            [jnp.where(first_half, r[2 * j * CHUNK:(2 * j + 1) * CHUNK], r[(2 * j + 1) * CHUNK:(2 * j + 2) * CHUNK])
             for j in range(GROUP_WIDTH // LANES)], axis=1)
        ya_sc[rq:rq + CHUNK, g * GROUP_WIDTH:(g + 1) * GROUP_WIDTH] = out.astype(BF16)

    def fill_bands():
        for s in range(SB):
            r0 = s * TT
            kb_sc[s, WINDOW:band_rows, :] = pq_sc[r0:r0 + TT, K0:K0 + KV_WIDTH]
            vb_sc[s, WINDOW:band_rows, :] = pq_sc[r0:r0 + TT, V0:V0 + KV_WIDTH]
            kb_sc[s, band_rows:, :] = jnp.zeros((BAND_PAD, KV_WIDTH), F32)
            vb_sc[s, band_rows:, :] = jnp.zeros((BAND_PAD, KV_WIDTH), F32)
            kw_ref[s] = kb_sc[s, TT:TT + WINDOW, :]
            vw_ref[s] = vb_sc[s, TT:TT + WINDOW, :]

    nw = nw_ref[...]

    def hgrn_outputs(hh):
        sl = slice(hh * HGRN_DV, (hh + 1) * HGRN_DV)
        for c in range(n_blocks):
            rows = slice(c * HGRN_BLOCK, (c + 1) * HGRN_BLOCK)
            o = _dot(a_sc[hh, rows, :], v_sc[hh, rows, :]) + _dot(qi_sc[hh, rows, :], s16_sc[hh, c])
            o = o * lax.rsqrt(jnp.mean(o * o, axis=-1, keepdims=True) + RMS_EPS)
            hg = pg_sc[rows, sl]
            yh_sc[rows, sl] = (o * nw[:, sl] * (hg * _sigmoid(hg))).astype(BF16)

    def interleave(units, fillers):
        for i, unit in enumerate(units):
            unit()
            for p in fillers[i * len(fillers) // len(units):(i + 1) * len(fillers) // len(units)]:
                p()

    def attn_group(s, g):
        for c in range(n_chunks):
            attn_chunk(s, g, c)

    for p in proj_h:
        p()
    interleave([functools.partial(prep_block, c) for c in range(n_blocks)], proj_q + proj_g)
    fill_bands()
    interleave([functools.partial(attn_group, s, g) for s in range(SB) for g in range(N_KV_HEADS)], proj_z)
    for hh in range(HGRN_HEADS):
        hgrn_outputs(hh)

    for s in range(SB):
        knext = kb_sc[s, TT:TT + WINDOW, :]
        vnext = vb_sc[s, TT:TT + WINDOW, :]
        kb_sc[s, 0:WINDOW, :] = knext
        vb_sc[s, 0:WINDOW, :] = vnext

    pa = _dot(ya_sc[...], wba_ref[...])
    ph = _dot(yh_sc[...], wbh_ref[...])
    merged = (_sigmoid(pz_sc[:, 0:D_MODEL]) * pa
              + _sigmoid(pz_sc[:, D_MODEL:2 * D_MODEL]) * ph)
    mo = _dot(merged.astype(BF16), wout_ref[...])
    for s in range(SB):
        g1 = mod_ref[s, 2:3, :]
        u = ALPHA * x_ref[s] + g1 * mo[s * TT:(s + 1) * TT]
        y_ref[s] = _ln_rows(u) * lng_ref[...] + lnb_ref[...]

    @pl.when(t == last_t)
    def _():
        for s in range(SB):
            for hh in range(HGRN_HEADS):
                sn_ref[s, hh] = s_sc[s, hh]


def _const_spec(shape):
    zeros = (0,) * len(shape)
    return pl.BlockSpec(shape, lambda i, t: zeros, pipeline_mode=pl.Buffered(1))


def _mixer_call(x, mod, ck, cv, st, w_in, wba, wbh, wout, sinks, lbl, nw, lng, lnb, *, SB, TT, pos0):
    nseq, T, _ = x.shape
    TM = SB * TT
    grid = (nseq // SB, T // TT)
    seq_map3 = lambda i, t: (i, 0, 0)
    seq_map4 = lambda i, t: (i, 0, 0, 0)
    kernel = functools.partial(_mixer_kernel, SB=SB, TT=TT, pos0=pos0)
    return pl.pallas_call(
        kernel,
        grid=grid,
        in_specs=[
            pl.BlockSpec((SB, TT, D_MODEL), lambda i, t: (i, t, 0)),
            pl.BlockSpec((SB, 6, D_MODEL), seq_map3),
            pl.BlockSpec((SB, WINDOW, KV_WIDTH), seq_map3),
            pl.BlockSpec((SB, WINDOW, KV_WIDTH), seq_map3),
            pl.BlockSpec((SB, HGRN_HEADS, HGRN_DK, HGRN_DV), seq_map4),
            _const_spec((D_MODEL, IN_WIDTH)),
            _const_spec((ATTN_WIDTH, D_MODEL)),
            _const_spec((HGRN_WIDTH, D_MODEL)),
            _const_spec((D_MODEL, D_MODEL)),
            pl.BlockSpec(memory_space=pltpu.SMEM),
            _const_spec((2, HGRN_WIDTH)),
            _const_spec((1, HGRN_WIDTH)),
            _const_spec((1, D_MODEL)),
            _const_spec((1, D_MODEL)),
        ],
        out_specs=[
            pl.BlockSpec((SB, TT, D_MODEL), lambda i, t: (i, t, 0)),
            pl.BlockSpec((SB, WINDOW, KV_WIDTH), seq_map3),
            pl.BlockSpec((SB, WINDOW, KV_WIDTH), seq_map3),
            pl.BlockSpec((SB, HGRN_HEADS, HGRN_DK, HGRN_DV), seq_map4),
        ],
        out_shape=[
            jax.ShapeDtypeStruct((nseq, T, D_MODEL), F32),
            jax.ShapeDtypeStruct((nseq, WINDOW, KV_WIDTH), F32),
            jax.ShapeDtypeStruct((nseq, WINDOW, KV_WIDTH), F32),
            jax.ShapeDtypeStruct((nseq, HGRN_HEADS, HGRN_DK, HGRN_DV), F32),
        ],
        scratch_shapes=[
            pltpu.VMEM((TM, D_MODEL), BF16),
            pltpu.VMEM((TM, HQ0 - Q0), F32),
            pltpu.VMEM((TM, HI0 - HQ0), F32),
            pltpu.VMEM((TM, HG0 - HI0), BF16),
            pltpu.VMEM((TM, GA0 - HG0), F32),
            pltpu.VMEM((TM, IN_WIDTH - GA0), F32),
            pltpu.VMEM((SB, WINDOW + TT + BAND_PAD, KV_WIDTH), F32),
            pltpu.VMEM((SB, WINDOW + TT + BAND_PAD, KV_WIDTH), F32),
            pltpu.VMEM((SB, HGRN_HEADS, HGRN_DK, HGRN_DV), F32),
            pltpu.VMEM((TM, ATTN_WIDTH), BF16),
            pltpu.VMEM((TM, HGRN_WIDTH), BF16),
            pltpu.VMEM((HGRN_HEADS, TM, HGRN_BLOCK), BF16),
            pltpu.VMEM((HGRN_HEADS, TM // HGRN_BLOCK, HGRN_DK, HGRN_DV), BF16),
            pltpu.VMEM((HGRN_HEADS, TM, HGRN_DK), BF16),
            pltpu.VMEM((HGRN_HEADS, TM, HGRN_DV), BF16),
        ],
        compiler_params=pltpu.CompilerParams(
            dimension_semantics=("arbitrary", "arbitrary"), vmem_limit_bytes=VMEM_LIMIT_BYTES),
        name="mixer",
    )(x, mod, ck, cv, st, w_in, wba, wbh, wout, sinks, lbl, nw, lng, lnb)


def _ffn_kernel(x_ref, mod_ref, wup_ref, wdn_ref, lng_ref, lnb_ref, y_ref, h_sc, act_sc, *, SB, TT):
    for s in range(SB):
        sh2 = mod_ref[s, 3:4, :]
        sc2 = mod_ref[s, 4:5, :]
        h = _ln_rows(x_ref[s]) * (1.0 + sc2) + sh2
        h_sc[s * TT:(s + 1) * TT, :] = h.astype(BF16)
    for j in range(D_FF // MXU_WIDTH):
        cols = slice(j * MXU_WIDTH, (j + 1) * MXU_WIDTH)
        gcols = slice(D_FF + j * MXU_WIDTH, D_FF + (j + 1) * MXU_WIDTH)
        u = _dot(h_sc[...], wup_ref[:, cols])
        gt = _dot(h_sc[...], wup_ref[:, gcols])
        act_sc[:, cols] = (gt * _sigmoid(gt) * u).astype(BF16)
    f = _dot(act_sc[...], wdn_ref[...])
    for s in range(SB):
        g2 = mod_ref[s, 5:6, :]
        u = ALPHA * x_ref[s] + g2 * f[s * TT:(s + 1) * TT]
        y_ref[s] = _ln_rows(u) * lng_ref[...] + lnb_ref[...]


def _ffn_call(x, mod, wup, wdn, lng, lnb, *, SB, TT):
    nseq, T, _ = x.shape
    TM = SB * TT
    kernel = functools.partial(_ffn_kernel, SB=SB, TT=TT)
    return pl.pallas_call(
        kernel,
        grid=(nseq // SB, T // TT),
        in_specs=[
            pl.BlockSpec((SB, TT, D_MODEL), lambda i, t: (i, t, 0)),
            pl.BlockSpec((SB, 6, D_MODEL), lambda i, t: (i, 0, 0)),
            _const_spec((D_MODEL, 2 * D_FF)),
            _const_spec((D_FF, D_MODEL)),
            _const_spec((1, D_MODEL)),
            _const_spec((1, D_MODEL)),
        ],
        out_specs=pl.BlockSpec((SB, TT, D_MODEL), lambda i, t: (i, t, 0)),
        out_shape=jax.ShapeDtypeStruct((nseq, T, D_MODEL), F32),
        scratch_shapes=[
            pltpu.VMEM((TM, D_MODEL), BF16),
            pltpu.VMEM((TM, D_FF), BF16),
        ],
        compiler_params=pltpu.CompilerParams(
            dimension_semantics=("arbitrary", "arbitrary"), vmem_limit_bytes=VMEM_LIMIT_BYTES),
        name="ffn",
    )(x, mod, wup, wdn, lng, lnb)


def kernel(x_prompt, x_sample, cache_attn_k, cache_attn_v, state_hgrn, c_prompt, c_sample, w_ada, b_ada, w_in, attn_sinks, hgrn_lb_logits, hgrn_norm_w, w_branch_attn, w_branch_hgrn, w_out, ln_mix_g, ln_mix_b, w_up, w_down, ln_ffn_g, ln_ffn_b):
    assert w_ada.shape[0] == DEPTH and hgrn_lb_logits.shape[0] == DEPTH + 1
    nb = x_prompt.shape[0]
    ns = x_sample.shape[0]
    assert nb + ns <= ADA_ROWS and x_sample.shape[1] == CHUNK

    c_all = jnp.concatenate([c_prompt, c_sample, jnp.zeros((ADA_ROWS - nb - ns, D_MODEL), F32)], axis=0)
    mod = _ada_call(c_all, w_ada[0], b_ada[0][None, :]).reshape(ADA_ROWS, 6, D_MODEL)
    mod_p = mod[:nb]
    mod_s = mod[nb:nb + ns]

    mixer_w = (w_in[0].astype(BF16), w_branch_attn[0].astype(BF16), w_branch_hgrn[0].astype(BF16),
               w_out[0].astype(BF16), attn_sinks[0], hgrn_lb_logits, hgrn_norm_w[0][None, :],
               ln_mix_g[0][None, :], ln_mix_b[0][None, :])
    ffn_w = (w_up[0].astype(BF16), w_down[0].astype(BF16), ln_ffn_g[0][None, :], ln_ffn_b[0][None, :])

    zk = jnp.zeros((nb, WINDOW, KV_WIDTH), F32)
    zs = jnp.zeros((nb, HGRN_HEADS, HGRN_DK, HGRN_DV), F32)
    x1p, kp, vp, sp = _mixer_call(x_prompt, mod_p, zk, zk, zs, *mixer_w, SB=1, TT=PROMPT_TILE, pos0=0)
    x1s, ks, vs, ss = _mixer_call(
        x_sample, mod_s,
        cache_attn_k[0].reshape(ns, WINDOW, KV_WIDTH), cache_attn_v[0].reshape(ns, WINDOW, KV_WIDTH),
        state_hgrn[0], *mixer_w, SB=MIXER_SAMPLE_SEQS, TT=CHUNK, pos0=PAST_LEN)

    yp = _ffn_call(x1p, mod_p, *ffn_w, SB=1, TT=PROMPT_TILE)
    ys = _ffn_call(x1s, mod_s, *ffn_w, SB=FFN_SAMPLE_SEQS, TT=CHUNK)

    win = lambda a: a.reshape(1, a.shape[0], WINDOW, N_KV_HEADS, HEAD_DIM)
    return (yp, ys, win(kp), win(vp), sp[None], win(ks), win(vs), ss[None])
```

```python
import functools

import jax
import jax.numpy as jnp
from jax import lax
from jax.experimental import pallas as pl
from jax.experimental.pallas import tpu as pltpu

D_MODEL = 1024
PAST_LEN = 1024
CHUNK = 64
WINDOW = 128
HEAD_DIM = 64
N_Q_HEADS = 8
N_KV_HEADS = 2
Q_GROUP = N_Q_HEADS // N_KV_HEADS
ATTN_WIDTH = N_Q_HEADS * HEAD_DIM
KV_WIDTH = N_KV_HEADS * HEAD_DIM
GROUP_WIDTH = Q_GROUP * HEAD_DIM
HGRN_HEADS = 8
HGRN_DK = 128
HGRN_DV = 128
HGRN_WIDTH = HGRN_HEADS * HGRN_DK
D_FF = 2816
IN_WIDTH = ATTN_WIDTH + 2 * KV_WIDTH + 4 * HGRN_WIDTH + 2 * D_MODEL
DEPTH = 1
ALPHA = (2 * DEPTH) ** 0.25
LN_EPS = 1e-5
RMS_EPS = 1e-6

Q0 = 0
K0 = Q0 + ATTN_WIDTH
V0 = K0 + KV_WIDTH
HQ0 = V0 + KV_WIDTH
HF0 = HQ0 + HGRN_WIDTH
HI0 = HF0 + HGRN_WIDTH
HG0 = HI0 + HGRN_WIDTH
GA0 = HG0 + HGRN_WIDTH
GH0 = GA0 + D_MODEL

LANES = 128
MXU_WIDTH = 256
VMEM_LIMIT_BYTES = 60 * 1024 * 1024

PROMPT_TILE = 512
MIXER_SAMPLE_SEQS = 4
FFN_SAMPLE_SEQS = 8

HGRN_BLOCK = 64
HGRN_HALF = HGRN_BLOCK // 2
KEY_WIN = 256
BAND_PAD = KEY_WIN - WINDOW - CHUNK
ADA_ROWS = 40
ADA_BLOCK = 1024

BF16 = jnp.bfloat16
F32 = jnp.float32


def _sigmoid(x):
    return 1.0 / (1.0 + jnp.exp(-x))


def _ln_rows(x):
    mu = jnp.mean(x, axis=-1, keepdims=True)
    xc = x - mu
    var = jnp.mean(xc * xc, axis=-1, keepdims=True)
    return xc * lax.rsqrt(var + LN_EPS)


def _dot(a, b):
    return jnp.dot(a, b, preferred_element_type=F32)


def _dot_nt(a, b):
    return lax.dot_general(a, b, (((1,), (1,)), ((), ())), preferred_element_type=F32)


def _cumsum_rows(x):
    n = x.shape[0]
    row = lax.broadcasted_iota(jnp.int32, x.shape, 0)
    s = 1
    while s < n:
        x = x + jnp.where(row >= s, pltpu.roll(x, s, axis=0), 0.0)
        s *= 2
    return x


def _ada_kernel(c_ref, w_ref, b_ref, o_ref):
    c = c_ref[...]
    a = c * _sigmoid(c)
    o_ref[...] = _dot(a.astype(BF16), w_ref[...].astype(BF16)) + b_ref[...]


def _ada_call(c_all, w_ada, b_ada):
    n = w_ada.shape[1]
    return pl.pallas_call(
        _ada_kernel,
        grid=(n // ADA_BLOCK,),
        in_specs=[
            pl.BlockSpec((ADA_ROWS, D_MODEL), lambda j: (0, 0)),
            pl.BlockSpec((D_MODEL, ADA_BLOCK), lambda j: (0, j)),
            pl.BlockSpec((1, ADA_BLOCK), lambda j: (0, j)),
        ],
        out_specs=pl.BlockSpec((ADA_ROWS, ADA_BLOCK), lambda j: (0, j)),
        out_shape=jax.ShapeDtypeStruct((ADA_ROWS, n), F32),
        compiler_params=pltpu.CompilerParams(dimension_semantics=("arbitrary",)),
        name="ada_mod",
    )(c_all, w_ada, b_ada)


def _mixer_kernel(x_ref, mod_ref, ck_ref, cv_ref, st_ref, w_in_ref, wba_ref, wbh_ref, wout_ref,
                  sinks_ref, lbl_ref, nw_ref, lng_ref, lnb_ref,
                  y_ref, kw_ref, vw_ref, sn_ref,
                  h_sc, pq_sc, ph_sc, pi_sc, pg_sc, pz_sc, kb_sc, vb_sc, s_sc, ya_sc, yh_sc,
                  a_sc, s16_sc, qi_sc, v_sc, m16_sc, *, SB, TT, pos0):
    t = pl.program_id(1)
    last_t = pl.num_programs(1) - 1
    n_chunks = TT // CHUNK
    band_rows = WINDOW + TT
    TM = SB * TT
    blocks_per_seq = TT // HGRN_BLOCK
    n_blocks = SB * blocks_per_seq

    @pl.when(t == 0)
    def _():
        for s in range(SB):
            kb_sc[s, 0:WINDOW, :] = ck_ref[s]
            vb_sc[s, 0:WINDOW, :] = cv_ref[s]
            for hh in range(HGRN_HEADS):
                s_sc[s, hh] = st_ref[s, hh]

    for s in range(SB):
        sh1 = mod_ref[s, 0:1, :]
        sc1 = mod_ref[s, 1:2, :]
        h = _ln_rows(x_ref[s]) * (1.0 + sc1) + sh1
        h_sc[s * TT:(s + 1) * TT, :] = h.astype(BF16)

    def project(dst, w_col0, j):
        cols = slice(j * MXU_WIDTH, (j + 1) * MXU_WIDTH)
        wcols = slice(w_col0 + j * MXU_WIDTH, w_col0 + (j + 1) * MXU_WIDTH)
        dst[:, cols] = _dot(h_sc[...], w_in_ref[:, wcols]).astype(dst.dtype)

    proj_q = [functools.partial(project, pq_sc, Q0, j) for j in range((HQ0 - Q0) // MXU_WIDTH)]
    proj_h = ([functools.partial(project, ph_sc, HQ0, j) for j in range((HI0 - HQ0) // MXU_WIDTH)]
              + [functools.partial(project, pi_sc, HI0, j) for j in range((HG0 - HI0) // MXU_WIDTH)])
    proj_g = [functools.partial(project, pg_sc, HG0, j) for j in range((GA0 - HG0) // MXU_WIDTH)]
    proj_z = [functools.partial(project, pz_sc, GA0, j) for j in range((IN_WIDTH - GA0) // MXU_WIDTH)]

    l0 = lbl_ref[0:1, :]
    l1 = lbl_ref[1:2, :]
    lmax = jnp.maximum(l0, l1)
    e0 = jnp.exp(l0 - lmax)
    lb = e0 / (e0 + jnp.exp(l1 - lmax))
    oml = 1.0 - lb
    in_h1 = lax.broadcasted_iota(jnp.int32, (HGRN_BLOCK, 1), 0) < HGRN_HALF
    causal = (lax.broadcasted_iota(jnp.int32, (HGRN_BLOCK, HGRN_BLOCK), 0)
              >= lax.broadcasted_iota(jnp.int32, (HGRN_BLOCK, HGRN_BLOCK), 1))
    m1 = HGRN_HALF // 2 - 1
    m2 = HGRN_HALF + m1

    def prep_block(c):
        rows = slice(c * HGRN_BLOCK, (c + 1) * HGRN_BLOCK)
        sg = _sigmoid(ph_sc[rows, HGRN_WIDTH:2 * HGRN_WIDTH])
        log_f = jnp.log(lb + oml * sg)
        k_in = oml * (1.0 - sg)
        b = _cumsum_rows(log_f)
        r1 = b[m1:m1 + 1, :]
        r2 = b[m2:m2 + 1, :]
        b_last = b[HGRN_BLOCK - 1:HGRN_BLOCK, :]
        rh = jnp.where(in_h1, r1, r2)
        hq = ph_sc[rows, 0:HGRN_WIDTH]
        qe = hq * _sigmoid(hq) * jnp.exp(b - rh)
        ke = k_in * jnp.exp(rh - b)
        qi = qe * jnp.where(in_h1, jnp.exp(r1), jnp.exp(r2))
        kd = ke * jnp.where(in_h1, jnp.exp(b_last - r1), jnp.exp(b_last - r2))
        la = jnp.where(in_h1, qe, qe * jnp.exp(r2 - r1))
        lb2 = jnp.where(in_h1, 0.0, qe)
        ra = jnp.where(in_h1, ke, 0.0)
        rb = jnp.where(in_h1, 0.0, ke)
        dec_col = jnp.exp(b_last).T
        la16, lb16, ra16, rb16, qi16, kdt16 = (a.astype(BF16) for a in (la, lb2, ra, rb, qi, kd.T))
        v16 = pi_sc[rows, :]
        s = c // blocks_per_seq
        for hh in range(HGRN_HEADS):
            sl = slice(hh * HGRN_DK, (hh + 1) * HGRN_DK)
            lhs = jnp.concatenate([la16[:, sl], lb16[:, sl]], axis=1)
            rhs = jnp.concatenate([ra16[:, sl], rb16[:, sl]], axis=1)
            a_sc[hh, rows, :] = jnp.where(causal, _dot_nt(lhs, rhs), 0.0).astype(BF16)
            qi_sc[hh, rows, :] = qi16[:, sl]
            v_sc[hh, rows, :] = v16[:, sl]
            st = s_sc[s, hh]
            s16_sc[hh, c] = st.astype(BF16)
            s_sc[s, hh] = dec_col[sl, :] * st + _dot(kdt16[sl, :], v16[:, sl])

    lane = lax.broadcasted_iota(jnp.int32, (1, LANES), 1)
    row = lax.broadcasted_iota(jnp.int32, (Q_GROUP * CHUNK, 1), 0)
    head_of_row = row >> 6
    q_of_row = row & (CHUNK - 1)
    key = lax.broadcasted_iota(jnp.int32, (1, KEY_WIN), 1)
    dist = jnp.abs(q_of_row + WINDOW - key).astype(F32)
    first_half = lane < HEAD_DIM

    @functools.cache
    def attn_operands(s, g):
        kband = kb_sc[s]
        vband = vb_sc[s]
        kroll = pltpu.roll(kband, HEAD_DIM, axis=1)
        vroll = pltpu.roll(vband, HEAD_DIM, axis=1)
        kk = jnp.where(first_half, kband if g == 0 else kroll, 0.0).astype(BF16)
        v1 = jnp.where(first_half, vband, vroll) if g == 0 else jnp.where(first_half, vroll, vband)
        vv = v1.astype(BF16)
        slope = jnp.zeros((Q_GROUP * CHUNK, 1), F32)
        sink = jnp.zeros((Q_GROUP * CHUNK, 1), F32)
        for hq in range(Q_GROUP):
            slope = jnp.where(head_of_row == hq, 2.0 ** -(g * Q_GROUP + hq + 1), slope)
            sink = jnp.where(head_of_row == hq, sinks_ref[g * Q_GROUP + hq], sink)
        return kk, vv, slope * dist, sink

    def attn_chunk(s, g, c):
        kk, vv, bias, sink = attn_operands(s, g)
        rq = s * TT + c * CHUNK
        qg = pq_sc[rq:rq + CHUNK, g * GROUP_WIDTH:(g + 1) * GROUP_WIDTH] * (HEAD_DIM ** -0.5)
        parts = []
        for j in range(GROUP_WIDTH // LANES):
            pair = qg[:, j * LANES:(j + 1) * LANES]
            parts.append(jnp.where(first_half, pair, 0.0))
            parts.append(jnp.where(first_half, pltpu.roll(pair, HEAD_DIM, axis=1), 0.0))
        lhs = jnp.concatenate(parts, axis=0).astype(BF16)
        kwin = kk[c * CHUNK:c * CHUNK + KEY_WIN]
        vwin = vv[c * CHUNK:c * CHUNK + KEY_WIN]
        sc = _dot_nt(lhs, kwin) - bias
        ok = key < WINDOW + CHUNK
        if pos0 < WINDOW:
            ok = ok & (key >= WINDOW - pos0 - c * CHUNK - t * TT)
        sc = jnp.where(ok, sc, -jnp.inf)
        m = jnp.maximum(jnp.max(sc, axis=-1, keepdims=True), sink)
        e = jnp.exp(sc - m)
        den = jnp.sum(e, axis=-1, keepdims=True) + jnp.exp(sink - m)
        r = _dot(e.astype(BF16), vwin) / den
        out = jnp.concatenate(
            [jnp.where(first_half, r[2 * j * CHUNK:(2 * j + 1) * CHUNK], r[(2 * j + 1) * CHUNK:(2 * j + 2) * CHUNK])
             for j in range(GROUP_WIDTH // LANES)], axis=1)
        ya_sc[rq:rq + CHUNK, g * GROUP_WIDTH:(g + 1) * GROUP_WIDTH] = out.astype(BF16)

    def fill_bands():
        for s in range(SB):
            r0 = s * TT
            kb_sc[s, WINDOW:band_rows, :] = pq_sc[r0:r0 + TT, K0:K0 + KV_WIDTH]
            vb_sc[s, WINDOW:band_rows, :] = pq_sc[r0:r0 + TT, V0:V0 + KV_WIDTH]
            kb_sc[s, band_rows:, :] = jnp.zeros((BAND_PAD, KV_WIDTH), F32)
            vb_sc[s, band_rows:, :] = jnp.zeros((BAND_PAD, KV_WIDTH), F32)
            kw_ref[s] = kb_sc[s, TT:TT + WINDOW, :]
            vw_ref[s] = vb_sc[s, TT:TT + WINDOW, :]

    nw = nw_ref[...]

    def hgrn_outputs(hh):
        sl = slice(hh * HGRN_DV, (hh + 1) * HGRN_DV)
        for c in range(n_blocks):
            rows = slice(c * HGRN_BLOCK, (c + 1) * HGRN_BLOCK)
            o = _dot(a_sc[hh, rows, :], v_sc[hh, rows, :]) + _dot(qi_sc[hh, rows, :], s16_sc[hh, c])
            o = o * lax.rsqrt(jnp.mean(o * o, axis=-1, keepdims=True) + RMS_EPS)
            hg = pg_sc[rows, sl]
            yh_sc[rows, sl] = (o * nw[:, sl] * (hg * _sigmoid(hg))).astype(BF16)

    def interleave(units, fillers):
        for i, unit in enumerate(units):
            unit()
            for p in fillers[i * len(fillers) // len(units):(i + 1) * len(fillers) // len(units)]:
                p()

    def attn_group(s, g):
        for c in range(n_chunks):
            attn_chunk(s, g, c)

    for p in proj_h:
        p()
    interleave([functools.partial(prep_block, c) for c in range(n_blocks)], proj_q + proj_g)
    fill_bands()
    interleave([functools.partial(attn_group, s, g) for s in range(SB) for g in range(N_KV_HEADS)], proj_z)
    for hh in range(HGRN_HEADS):
        hgrn_outputs(hh)

    for s in range(SB):
        knext = kb_sc[s, TT:TT + WINDOW, :]
        vnext = vb_sc[s, TT:TT + WINDOW, :]
        kb_sc[s, 0:WINDOW, :] = knext
        vb_sc[s, 0:WINDOW, :] = vnext

    for j in range(D_MODEL // MXU_WIDTH):
        cols = slice(j * MXU_WIDTH, (j + 1) * MXU_WIDTH)
        gcols = slice(D_MODEL + j * MXU_WIDTH, D_MODEL + (j + 1) * MXU_WIDTH)
        pa = _dot(ya_sc[...], wba_ref[:, cols])
        ph = _dot(yh_sc[...], wbh_ref[:, cols])
        merged = _sigmoid(pz_sc[:, cols]) * pa + _sigmoid(pz_sc[:, gcols]) * ph
        m16_sc[:, cols] = merged.astype(BF16)
    mo = _dot(m16_sc[...], wout_ref[...])
    for s in range(SB):
        g1 = mod_ref[s, 2:3, :]
        u = ALPHA * x_ref[s] + g1 * mo[s * TT:(s + 1) * TT]
        y_ref[s] = _ln_rows(u) * lng_ref[...] + lnb_ref[...]

    @pl.when(t == last_t)
    def _():
        for s in range(SB):
            for hh in range(HGRN_HEADS):
                sn_ref[s, hh] = s_sc[s, hh]


def _const_spec(shape):
    zeros = (0,) * len(shape)
    return pl.BlockSpec(shape, lambda i, t: zeros, pipeline_mode=pl.Buffered(1))


def _mixer_call(x, mod, ck, cv, st, w_in, wba, wbh, wout, sinks, lbl, nw, lng, lnb, *, SB, TT, pos0):
    nseq, T, _ = x.shape
    TM = SB * TT
    grid = (nseq // SB, T // TT)
    seq_map3 = lambda i, t: (i, 0, 0)
    seq_map4 = lambda i, t: (i, 0, 0, 0)
    kernel = functools.partial(_mixer_kernel, SB=SB, TT=TT, pos0=pos0)
    return pl.pallas_call(
        kernel,
        grid=grid,
        in_specs=[
            pl.BlockSpec((SB, TT, D_MODEL), lambda i, t: (i, t, 0)),
            pl.BlockSpec((SB, 6, D_MODEL), seq_map3),
            pl.BlockSpec((SB, WINDOW, KV_WIDTH), seq_map3),
            pl.BlockSpec((SB, WINDOW, KV_WIDTH), seq_map3),
            pl.BlockSpec((SB, HGRN_HEADS, HGRN_DK, HGRN_DV), seq_map4),
            _const_spec((D_MODEL, IN_WIDTH)),
            _const_spec((ATTN_WIDTH, D_MODEL)),
            _const_spec((HGRN_WIDTH, D_MODEL)),
            _const_spec((D_MODEL, D_MODEL)),
            pl.BlockSpec(memory_space=pltpu.SMEM),
            _const_spec((2, HGRN_WIDTH)),
            _const_spec((1, HGRN_WIDTH)),
            _const_spec((1, D_MODEL)),
            _const_spec((1, D_MODEL)),
        ],
        out_specs=[
            pl.BlockSpec((SB, TT, D_MODEL), lambda i, t: (i, t, 0)),
            pl.BlockSpec((SB, WINDOW, KV_WIDTH), seq_map3),
            pl.BlockSpec((SB, WINDOW, KV_WIDTH), seq_map3),
            pl.BlockSpec((SB, HGRN_HEADS, HGRN_DK, HGRN_DV), seq_map4),
        ],
        out_shape=[
            jax.ShapeDtypeStruct((nseq, T, D_MODEL), F32),
            jax.ShapeDtypeStruct((nseq, WINDOW, KV_WIDTH), F32),
            jax.ShapeDtypeStruct((nseq, WINDOW, KV_WIDTH), F32),
            jax.ShapeDtypeStruct((nseq, HGRN_HEADS, HGRN_DK, HGRN_DV), F32),
        ],
        scratch_shapes=[
            pltpu.VMEM((TM, D_MODEL), BF16),
            pltpu.VMEM((TM, HQ0 - Q0), F32),
            pltpu.VMEM((TM, HI0 - HQ0), F32),
            pltpu.VMEM((TM, HG0 - HI0), BF16),
            pltpu.VMEM((TM, GA0 - HG0), F32),
            pltpu.VMEM((TM, IN_WIDTH - GA0), F32),
            pltpu.VMEM((SB, WINDOW + TT + BAND_PAD, KV_WIDTH), F32),
            pltpu.VMEM((SB, WINDOW + TT + BAND_PAD, KV_WIDTH), F32),
            pltpu.VMEM((SB, HGRN_HEADS, HGRN_DK, HGRN_DV), F32),
            pltpu.VMEM((TM, ATTN_WIDTH), BF16),
            pltpu.VMEM((TM, HGRN_WIDTH), BF16),
            pltpu.VMEM((HGRN_HEADS, TM, HGRN_BLOCK), BF16),
            pltpu.VMEM((HGRN_HEADS, TM // HGRN_BLOCK, HGRN_DK, HGRN_DV), BF16),
            pltpu.VMEM((HGRN_HEADS, TM, HGRN_DK), BF16),
            pltpu.VMEM((HGRN_HEADS, TM, HGRN_DV), BF16),
            pltpu.VMEM((TM, D_MODEL), BF16),
        ],
        compiler_params=pltpu.CompilerParams(
            dimension_semantics=("arbitrary", "arbitrary"), vmem_limit_bytes=VMEM_LIMIT_BYTES),
        name="mixer",
    )(x, mod, ck, cv, st, w_in, wba, wbh, wout, sinks, lbl, nw, lng, lnb)


def _ffn_kernel(x_ref, mod_ref, wup_ref, wdn_ref, lng_ref, lnb_ref, y_ref, h_sc, act_sc, *, SB, TT):
    for s in range(SB):
        sh2 = mod_ref[s, 3:4, :]
        sc2 = mod_ref[s, 4:5, :]
        h = _ln_rows(x_ref[s]) * (1.0 + sc2) + sh2
        h_sc[s * TT:(s + 1) * TT, :] = h.astype(BF16)
    for j in range(D_FF // MXU_WIDTH):
        cols = slice(j * MXU_WIDTH, (j + 1) * MXU_WIDTH)
        gcols = slice(D_FF + j * MXU_WIDTH, D_FF + (j + 1) * MXU_WIDTH)
        u = _dot(h_sc[...], wup_ref[:, cols])
        gt = _dot(h_sc[...], wup_ref[:, gcols])
        act_sc[:, cols] = (gt * _sigmoid(gt) * u).astype(BF16)
    f = _dot(act_sc[...], wdn_ref[...])
    for s in range(SB):
        g2 = mod_ref[s, 5:6, :]
        u = ALPHA * x_ref[s] + g2 * f[s * TT:(s + 1) * TT]
        y_ref[s] = _ln_rows(u) * lng_ref[...] + lnb_ref[...]


def _ffn_call(x, mod, wup, wdn, lng, lnb, *, SB, TT):
    nseq, T, _ = x.shape
    TM = SB * TT
    kernel = functools.partial(_ffn_kernel, SB=SB, TT=TT)
    return pl.pallas_call(
        kernel,
        grid=(nseq // SB, T // TT),
        in_specs=[
            pl.BlockSpec((SB, TT, D_MODEL), lambda i, t: (i, t, 0)),
            pl.BlockSpec((SB, 6, D_MODEL), lambda i, t: (i, 0, 0)),
            _const_spec((D_MODEL, 2 * D_FF)),
            _const_spec((D_FF, D_MODEL)),
            _const_spec((1, D_MODEL)),
            _const_spec((1, D_MODEL)),
        ],
        out_specs=pl.BlockSpec((SB, TT, D_MODEL), lambda i, t: (i, t, 0)),
        out_shape=jax.ShapeDtypeStruct((nseq, T, D_MODEL), F32),
        scratch_shapes=[
            pltpu.VMEM((TM, D_MODEL), BF16),
            pltpu.VMEM((TM, D_FF), BF16),
        ],
        compiler_params=pltpu.CompilerParams(
            dimension_semantics=("arbitrary", "arbitrary"), vmem_limit_bytes=VMEM_LIMIT_BYTES),
        name="ffn",
    )(x, mod, wup, wdn, lng, lnb)


def kernel(x_prompt, x_sample, cache_attn_k, cache_attn_v, state_hgrn, c_prompt, c_sample, w_ada, b_ada, w_in, attn_sinks, hgrn_lb_logits, hgrn_norm_w, w_branch_attn, w_branch_hgrn, w_out, ln_mix_g, ln_mix_b, w_up, w_down, ln_ffn_g, ln_ffn_b):
    assert w_ada.shape[0] == DEPTH and hgrn_lb_logits.shape[0] == DEPTH + 1
    nb = x_prompt.shape[0]
    ns = x_sample.shape[0]
    assert nb + ns <= ADA_ROWS and x_sample.shape[1] == CHUNK

    c_all = jnp.concatenate([c_prompt, c_sample, jnp.zeros((ADA_ROWS - nb - ns, D_MODEL), F32)], axis=0)
    mod = _ada_call(c_all, w_ada[0], b_ada[0][None, :]).reshape(ADA_ROWS, 6, D_MODEL)
    mod_p = mod[:nb]
    mod_s = mod[nb:nb + ns]

    mixer_w = (w_in[0].astype(BF16), w_branch_attn[0].astype(BF16), w_branch_hgrn[0].astype(BF16),
               w_out[0].astype(BF16), attn_sinks[0], hgrn_lb_logits, hgrn_norm_w[0][None, :],
               ln_mix_g[0][None, :], ln_mix_b[0][None, :])
    ffn_w = (w_up[0].astype(BF16), w_down[0].astype(BF16), ln_ffn_g[0][None, :], ln_ffn_b[0][None, :])

    zk = jnp.zeros((nb, WINDOW, KV_WIDTH), F32)
    zs = jnp.zeros((nb, HGRN_HEADS, HGRN_DK, HGRN_DV), F32)
    x1p, kp, vp, sp = _mixer_call(x_prompt, mod_p, zk, zk, zs, *mixer_w, SB=1, TT=PROMPT_TILE, pos0=0)
    x1s, ks, vs, ss = _mixer_call(
        x_sample, mod_s,
        cache_attn_k[0].reshape(ns, WINDOW, KV_WIDTH), cache_attn_v[0].reshape(ns, WINDOW, KV_WIDTH),
        state_hgrn[0], *mixer_w, SB=MIXER_SAMPLE_SEQS, TT=CHUNK, pos0=PAST_LEN)

    yp = _ffn_call(x1p, mod_p, *ffn_w, SB=1, TT=PROMPT_TILE)
    ys = _ffn_call(x1s, mod_s, *ffn_w, SB=FFN_SAMPLE_SEQS, TT=CHUNK)

    win = lambda a: a.reshape(1, a.shape[0], WINDOW, N_KV_HEADS, HEAD_DIM)
    return (yp, ys, win(kp), win(vp), sp[None], win(ks), win(vs), ss[None])
```

```python
import functools

import jax
import jax.numpy as jnp
from jax import lax
from jax.experimental import pallas as pl
from jax.experimental.pallas import tpu as pltpu

D_MODEL = 1024
PAST_LEN = 1024
CHUNK = 64
WINDOW = 128
HEAD_DIM = 64
N_Q_HEADS = 8
N_KV_HEADS = 2
Q_GROUP = N_Q_HEADS // N_KV_HEADS
ATTN_WIDTH = N_Q_HEADS * HEAD_DIM
KV_WIDTH = N_KV_HEADS * HEAD_DIM
GROUP_WIDTH = Q_GROUP * HEAD_DIM
HGRN_HEADS = 8
HGRN_DK = 128
HGRN_DV = 128
HGRN_WIDTH = HGRN_HEADS * HGRN_DK
D_FF = 2816
IN_WIDTH = ATTN_WIDTH + 2 * KV_WIDTH + 4 * HGRN_WIDTH + 2 * D_MODEL
DEPTH = 1
ALPHA = (2 * DEPTH) ** 0.25
LN_EPS = 1e-5
RMS_EPS = 1e-6

Q0 = 0
K0 = Q0 + ATTN_WIDTH
V0 = K0 + KV_WIDTH
HQ0 = V0 + KV_WIDTH
HF0 = HQ0 + HGRN_WIDTH
HI0 = HF0 + HGRN_WIDTH
HG0 = HI0 + HGRN_WIDTH
GA0 = HG0 + HGRN_WIDTH
GH0 = GA0 + D_MODEL

LANES = 128
MXU_WIDTH = 256
VMEM_LIMIT_BYTES = 60 * 1024 * 1024

PROMPT_TILE = 512
MIXER_SAMPLE_SEQS = 4
FFN_SAMPLE_SEQS = 8

HGRN_BLOCK = 64
HGRN_HALF = HGRN_BLOCK // 2
KEY_WIN = 256
BAND_PAD = KEY_WIN - WINDOW - CHUNK
ADA_ROWS = 40
ADA_BLOCK = 1024

BF16 = jnp.bfloat16
F32 = jnp.float32


def _sigmoid(x):
    return 1.0 / (1.0 + jnp.exp(-x))


def _ln_rows(x):
    mu = jnp.mean(x, axis=-1, keepdims=True)
    xc = x - mu
    var = jnp.mean(xc * xc, axis=-1, keepdims=True)
    return xc * lax.rsqrt(var + LN_EPS)


def _dot(a, b):
    return jnp.dot(a, b, preferred_element_type=F32)


def _dot_nt(a, b):
    return lax.dot_general(a, b, (((1,), (1,)), ((), ())), preferred_element_type=F32)


def _cumsum_rows(x):
    n = x.shape[0]
    row = lax.broadcasted_iota(jnp.int32, x.shape, 0)
    s = 1
    while s < n:
        x = x + jnp.where(row >= s, pltpu.roll(x, s, axis=0), 0.0)
        s *= 2
    return x


def _ada_kernel(c_ref, w_ref, b_ref, o_ref):
    c = c_ref[...]
    a = c * _sigmoid(c)
    o_ref[...] = _dot(a.astype(BF16), w_ref[...].astype(BF16)) + b_ref[...]


def _ada_call(c_all, w_ada, b_ada):
    n = w_ada.shape[1]
    return pl.pallas_call(
        _ada_kernel,
        grid=(n // ADA_BLOCK,),
        in_specs=[
            pl.BlockSpec((ADA_ROWS, D_MODEL), lambda j: (0, 0)),
            pl.BlockSpec((D_MODEL, ADA_BLOCK), lambda j: (0, j)),
            pl.BlockSpec((1, ADA_BLOCK), lambda j: (0, j)),
        ],
        out_specs=pl.BlockSpec((ADA_ROWS, ADA_BLOCK), lambda j: (0, j)),
        out_shape=jax.ShapeDtypeStruct((ADA_ROWS, n), F32),
        compiler_params=pltpu.CompilerParams(dimension_semantics=("arbitrary",)),
        name="ada_mod",
    )(c_all, w_ada, b_ada)


def _mixer_kernel(x_ref, mod_ref, ck_ref, cv_ref, st_ref, w_in_ref, wba_ref, wbh_ref, wout_ref,
                  sinks_ref, lbl_ref, nw_ref, lng_ref, lnb_ref,
                  y_ref, kw_ref, vw_ref, sn_ref,
                  h_sc, pq_sc, ph_sc, pi_sc, pg_sc, pz_sc, kb_sc, vb_sc, s_sc, ya_sc, yh_sc,
                  a_sc, s16_sc, qi_sc, v_sc, m16_sc, *, SB, TT, pos0):
    t = pl.program_id(1)
    last_t = pl.num_programs(1) - 1
    n_chunks = TT // CHUNK
    band_rows = WINDOW + TT
    TM = SB * TT
    blocks_per_seq = TT // HGRN_BLOCK
    n_blocks = SB * blocks_per_seq

    @pl.when(t == 0)
    def _():
        for s in range(SB):
            kb_sc[s, 0:WINDOW, :] = ck_ref[s]
            vb_sc[s, 0:WINDOW, :] = cv_ref[s]
            for hh in range(HGRN_HEADS):
                s_sc[s, hh] = st_ref[s, hh]

    for s in range(SB):
        sh1 = mod_ref[s, 0:1, :]
        sc1 = mod_ref[s, 1:2, :]
        h = _ln_rows(x_ref[s]) * (1.0 + sc1) + sh1
        h_sc[s * TT:(s + 1) * TT, :] = h.astype(BF16)

    def project(dst, w_col0, j, act=None):
        cols = slice(j * MXU_WIDTH, (j + 1) * MXU_WIDTH)
        wcols = slice(w_col0 + j * MXU_WIDTH, w_col0 + (j + 1) * MXU_WIDTH)
        res = _dot(h_sc[...], w_in_ref[:, wcols])
        dst[:, cols] = (res if act is None else act(res)).astype(dst.dtype)

    proj_q = [functools.partial(project, pq_sc, Q0, j) for j in range((HQ0 - Q0) // MXU_WIDTH)]
    proj_h = ([functools.partial(project, ph_sc, HQ0, j) for j in range((HI0 - HQ0) // MXU_WIDTH)]
              + [functools.partial(project, pi_sc, HI0, j) for j in range((HG0 - HI0) // MXU_WIDTH)])
    proj_g = [functools.partial(project, pg_sc, HG0, j, lambda g: g * _sigmoid(g))
              for j in range((GA0 - HG0) // MXU_WIDTH)]
    proj_z = [functools.partial(project, pz_sc, GA0, j, _sigmoid) for j in range((IN_WIDTH - GA0) // MXU_WIDTH)]

    l0 = lbl_ref[0:1, :]
    l1 = lbl_ref[1:2, :]
    lmax = jnp.maximum(l0, l1)
    e0 = jnp.exp(l0 - lmax)
    lb = e0 / (e0 + jnp.exp(l1 - lmax))
    oml = 1.0 - lb
    in_h1 = lax.broadcasted_iota(jnp.int32, (HGRN_BLOCK, 1), 0) < HGRN_HALF
    causal = (lax.broadcasted_iota(jnp.int32, (HGRN_BLOCK, HGRN_BLOCK), 0)
              >= lax.broadcasted_iota(jnp.int32, (HGRN_BLOCK, HGRN_BLOCK), 1))
    m1 = HGRN_HALF // 2 - 1
    m2 = HGRN_HALF + m1

    def prep_block(c):
        rows = slice(c * HGRN_BLOCK, (c + 1) * HGRN_BLOCK)
        sg = _sigmoid(ph_sc[rows, HGRN_WIDTH:2 * HGRN_WIDTH])
        log_f = jnp.log(lb + oml * sg)
        k_in = oml * (1.0 - sg)
        b = _cumsum_rows(log_f)
        r1 = b[m1:m1 + 1, :]
        r2 = b[m2:m2 + 1, :]
        b_last = b[HGRN_BLOCK - 1:HGRN_BLOCK, :]
        rh = jnp.where(in_h1, r1, r2)
        hq = ph_sc[rows, 0:HGRN_WIDTH]
        qe = hq * _sigmoid(hq) * jnp.exp(b - rh)
        ke = k_in * jnp.exp(rh - b)
        qi = qe * jnp.where(in_h1, jnp.exp(r1), jnp.exp(r2))
        kd = ke * jnp.where(in_h1, jnp.exp(b_last - r1), jnp.exp(b_last - r2))
        la = jnp.where(in_h1, qe, qe * jnp.exp(r2 - r1))
        lb2 = jnp.where(in_h1, 0.0, qe)
        ra = jnp.where(in_h1, ke, 0.0)
        rb = jnp.where(in_h1, 0.0, ke)
        dec_col = jnp.exp(b_last).T
        la16, lb16, ra16, rb16, qi16, kdt16 = (a.astype(BF16) for a in (la, lb2, ra, rb, qi, kd.T))
        v16 = pi_sc[rows, :]
        s = c // blocks_per_seq
        for hh in range(HGRN_HEADS):
            sl = slice(hh * HGRN_DK, (hh + 1) * HGRN_DK)
            lhs = jnp.concatenate([la16[:, sl], lb16[:, sl]], axis=1)
            rhs = jnp.concatenate([ra16[:, sl], rb16[:, sl]], axis=1)
            a_sc[hh, rows, :] = jnp.where(causal, _dot_nt(lhs, rhs), 0.0).astype(BF16)
            qi_sc[hh, rows, :] = qi16[:, sl]
            v_sc[hh, rows, :] = v16[:, sl]
            st = s_sc[s, hh]
            s16_sc[hh, c] = st.astype(BF16)
            s_sc[s, hh] = dec_col[sl, :] * st + _dot(kdt16[sl, :], v16[:, sl])

    lane = lax.broadcasted_iota(jnp.int32, (1, LANES), 1)
    row = lax.broadcasted_iota(jnp.int32, (Q_GROUP * CHUNK, 1), 0)
    head_of_row = row >> 6
    q_of_row = row & (CHUNK - 1)
    key = lax.broadcasted_iota(jnp.int32, (1, KEY_WIN), 1)
    dist = jnp.abs(q_of_row + WINDOW - key).astype(F32)
    first_half = lane < HEAD_DIM

    @functools.cache
    def attn_operands(s, g):
        kband = kb_sc[s]
        vband = vb_sc[s]
        kroll = pltpu.roll(kband, HEAD_DIM, axis=1)
        vroll = pltpu.roll(vband, HEAD_DIM, axis=1)
        kk = jnp.where(first_half, kband if g == 0 else kroll, 0.0).astype(BF16)
        v1 = jnp.where(first_half, vband, vroll) if g == 0 else jnp.where(first_half, vroll, vband)
        vv = v1.astype(BF16)
        slope = jnp.zeros((Q_GROUP * CHUNK, 1), F32)
        sink = jnp.zeros((Q_GROUP * CHUNK, 1), F32)
        for hq in range(Q_GROUP):
            slope = jnp.where(head_of_row == hq, 2.0 ** -(g * Q_GROUP + hq + 1), slope)
            sink = jnp.where(head_of_row == hq, sinks_ref[g * Q_GROUP + hq], sink)
        return kk, vv, slope * dist, sink

    def attn_chunk(s, g, c):
        kk, vv, bias, sink = attn_operands(s, g)
        rq = s * TT + c * CHUNK
        qg = pq_sc[rq:rq + CHUNK, g * GROUP_WIDTH:(g + 1) * GROUP_WIDTH] * (HEAD_DIM ** -0.5)
        parts = []
        for j in range(GROUP_WIDTH // LANES):
            pair = qg[:, j * LANES:(j + 1) * LANES]
            parts.append(jnp.where(first_half, pair, 0.0))
            parts.append(jnp.where(first_half, pltpu.roll(pair, HEAD_DIM, axis=1), 0.0))
        lhs = jnp.concatenate(parts, axis=0).astype(BF16)
        kwin = kk[c * CHUNK:c * CHUNK + KEY_WIN]
        vwin = vv[c * CHUNK:c * CHUNK + KEY_WIN]
        sc = _dot_nt(lhs, kwin) - bias
        ok = key < WINDOW + CHUNK
        if pos0 < WINDOW:
            ok = ok & (key >= WINDOW - pos0 - c * CHUNK - t * TT)
        sc = jnp.where(ok, sc, -jnp.inf)
        m = jnp.maximum(jnp.max(sc, axis=-1, keepdims=True), sink)
        e = jnp.exp(sc - m)
        den = jnp.sum(e, axis=-1, keepdims=True) + jnp.exp(sink - m)
        r = _dot(e.astype(BF16), vwin) / den
        out = jnp.concatenate(
            [jnp.where(first_half, r[2 * j * CHUNK:(2 * j + 1) * CHUNK], r[(2 * j + 1) * CHUNK:(2 * j + 2) * CHUNK])
             for j in range(GROUP_WIDTH // LANES)], axis=1)
        ya_sc[rq:rq + CHUNK, g * GROUP_WIDTH:(g + 1) * GROUP_WIDTH] = out.astype(BF16)

    def fill_bands():
        for s in range(SB):
            r0 = s * TT
            kb_sc[s, WINDOW:band_rows, :] = pq_sc[r0:r0 + TT, K0:K0 + KV_WIDTH]
            vb_sc[s, WINDOW:band_rows, :] = pq_sc[r0:r0 + TT, V0:V0 + KV_WIDTH]
            kb_sc[s, band_rows:, :] = jnp.zeros((BAND_PAD, KV_WIDTH), F32)
            vb_sc[s, band_rows:, :] = jnp.zeros((BAND_PAD, KV_WIDTH), F32)
            kw_ref[s] = kb_sc[s, TT:TT + WINDOW, :]
            vw_ref[s] = vb_sc[s, TT:TT + WINDOW, :]

    nw = nw_ref[...]

    def hgrn_outputs(hh):
        sl = slice(hh * HGRN_DV, (hh + 1) * HGRN_DV)
        for c in range(n_blocks):
            rows = slice(c * HGRN_BLOCK, (c + 1) * HGRN_BLOCK)
            o = _dot(a_sc[hh, rows, :], v_sc[hh, rows, :]) + _dot(qi_sc[hh, rows, :], s16_sc[hh, c])
            o = o * lax.rsqrt(jnp.mean(o * o, axis=-1, keepdims=True) + RMS_EPS)
            yh_sc[rows, sl] = (o * nw[:, sl] * pg_sc[rows, sl].astype(F32)).astype(BF16)

    def interleave(units, fillers):
        for i, unit in enumerate(units):
            unit()
            for p in fillers[i * len(fillers) // len(units):(i + 1) * len(fillers) // len(units)]:
                p()

    def attn_group(s, g):
        for c in range(n_chunks):
            attn_chunk(s, g, c)

    for p in proj_h:
        p()
    interleave([functools.partial(prep_block, c) for c in range(n_blocks)], proj_q + proj_g)
    fill_bands()
    interleave([functools.partial(attn_group, s, g) for s in range(SB) for g in range(N_KV_HEADS)], proj_z)
    for hh in range(HGRN_HEADS):
        hgrn_outputs(hh)

    for s in range(SB):
        knext = kb_sc[s, TT:TT + WINDOW, :]
        vnext = vb_sc[s, TT:TT + WINDOW, :]
        kb_sc[s, 0:WINDOW, :] = knext
        vb_sc[s, 0:WINDOW, :] = vnext

    for j in range(D_MODEL // MXU_WIDTH):
        cols = slice(j * MXU_WIDTH, (j + 1) * MXU_WIDTH)
        gcols = slice(D_MODEL + j * MXU_WIDTH, D_MODEL + (j + 1) * MXU_WIDTH)
        pa = _dot(ya_sc[...], wba_ref[:, cols])
        ph = _dot(yh_sc[...], wbh_ref[:, cols])
        merged = pz_sc[:, cols].astype(F32) * pa + pz_sc[:, gcols].astype(F32) * ph
        m16_sc[:, cols] = merged.astype(BF16)
    mo = _dot(m16_sc[...], wout_ref[...])
    for s in range(SB):
        g1 = mod_ref[s, 2:3, :]
        u = ALPHA * x_ref[s] + g1 * mo[s * TT:(s + 1) * TT]
        y_ref[s] = _ln_rows(u) * lng_ref[...] + lnb_ref[...]

    @pl.when(t == last_t)
    def _():
        for s in range(SB):
            for hh in range(HGRN_HEADS):
                sn_ref[s, hh] = s_sc[s, hh]


def _const_spec(shape):
    zeros = (0,) * len(shape)
    return pl.BlockSpec(shape, lambda i, t: zeros, pipeline_mode=pl.Buffered(1))


def _mixer_call(x, mod, ck, cv, st, w_in, wba, wbh, wout, sinks, lbl, nw, lng, lnb, *, SB, TT, pos0):
    nseq, T, _ = x.shape
    TM = SB * TT
    grid = (nseq // SB, T // TT)
    seq_map3 = lambda i, t: (i, 0, 0)
    seq_map4 = lambda i, t: (i, 0, 0, 0)
    kernel = functools.partial(_mixer_kernel, SB=SB, TT=TT, pos0=pos0)
    return pl.pallas_call(
        kernel,
        grid=grid,
        in_specs=[
            pl.BlockSpec((SB, TT, D_MODEL), lambda i, t: (i, t, 0)),
            pl.BlockSpec((SB, 6, D_MODEL), seq_map3),
            pl.BlockSpec((SB, WINDOW, KV_WIDTH), seq_map3),
            pl.BlockSpec((SB, WINDOW, KV_WIDTH), seq_map3),
            pl.BlockSpec((SB, HGRN_HEADS, HGRN_DK, HGRN_DV), seq_map4),
            _const_spec((D_MODEL, IN_WIDTH)),
            _const_spec((ATTN_WIDTH, D_MODEL)),
            _const_spec((HGRN_WIDTH, D_MODEL)),
            _const_spec((D_MODEL, D_MODEL)),
            pl.BlockSpec(memory_space=pltpu.SMEM),
            _const_spec((2, HGRN_WIDTH)),
            _const_spec((1, HGRN_WIDTH)),
            _const_spec((1, D_MODEL)),
            _const_spec((1, D_MODEL)),
        ],
        out_specs=[
            pl.BlockSpec((SB, TT, D_MODEL), lambda i, t: (i, t, 0)),
            pl.BlockSpec((SB, WINDOW, KV_WIDTH), seq_map3),
            pl.BlockSpec((SB, WINDOW, KV_WIDTH), seq_map3),
            pl.BlockSpec((SB, HGRN_HEADS, HGRN_DK, HGRN_DV), seq_map4),
        ],
        out_shape=[
            jax.ShapeDtypeStruct((nseq, T, D_MODEL), F32),
            jax.ShapeDtypeStruct((nseq, WINDOW, KV_WIDTH), F32),
            jax.ShapeDtypeStruct((nseq, WINDOW, KV_WIDTH), F32),
            jax.ShapeDtypeStruct((nseq, HGRN_HEADS, HGRN_DK, HGRN_DV), F32),
        ],
        scratch_shapes=[
            pltpu.VMEM((TM, D_MODEL), BF16),
            pltpu.VMEM((TM, HQ0 - Q0), F32),
            pltpu.VMEM((TM, HI0 - HQ0), F32),
            pltpu.VMEM((TM, HG0 - HI0), BF16),
            pltpu.VMEM((TM, GA0 - HG0), BF16),
            pltpu.VMEM((TM, IN_WIDTH - GA0), BF16),
            pltpu.VMEM((SB, WINDOW + TT + BAND_PAD, KV_WIDTH), F32),
            pltpu.VMEM((SB, WINDOW + TT + BAND_PAD, KV_WIDTH), F32),
            pltpu.VMEM((SB, HGRN_HEADS, HGRN_DK, HGRN_DV), F32),
            pltpu.VMEM((TM, ATTN_WIDTH), BF16),
            pltpu.VMEM((TM, HGRN_WIDTH), BF16),
            pltpu.VMEM((HGRN_HEADS, TM, HGRN_BLOCK), BF16),
            pltpu.VMEM((HGRN_HEADS, TM // HGRN_BLOCK, HGRN_DK, HGRN_DV), BF16),
            pltpu.VMEM((HGRN_HEADS, TM, HGRN_DK), BF16),
            pltpu.VMEM((HGRN_HEADS, TM, HGRN_DV), BF16),
            pltpu.VMEM((TM, D_MODEL), BF16),
        ],
        compiler_params=pltpu.CompilerParams(
            dimension_semantics=("arbitrary", "arbitrary"), vmem_limit_bytes=VMEM_LIMIT_BYTES),
        name="mixer",
    )(x, mod, ck, cv, st, w_in, wba, wbh, wout, sinks, lbl, nw, lng, lnb)


def _ffn_kernel(x_ref, mod_ref, wup_ref, wdn_ref, lng_ref, lnb_ref, y_ref, h_sc, act_sc, *, SB, TT):
    for s in range(SB):
        sh2 = mod_ref[s, 3:4, :]
        sc2 = mod_ref[s, 4:5, :]
        h = _ln_rows(x_ref[s]) * (1.0 + sc2) + sh2
        h_sc[s * TT:(s + 1) * TT, :] = h.astype(BF16)
    for j in range(D_FF // MXU_WIDTH):
        cols = slice(j * MXU_WIDTH, (j + 1) * MXU_WIDTH)
        gcols = slice(D_FF + j * MXU_WIDTH, D_FF + (j + 1) * MXU_WIDTH)
        u = _dot(h_sc[...], wup_ref[:, cols])
        gt = _dot(h_sc[...], wup_ref[:, gcols])
        act_sc[:, cols] = (gt * _sigmoid(gt) * u).astype(BF16)
    f = _dot(act_sc[...], wdn_ref[...])
    for s in range(SB):
        g2 = mod_ref[s, 5:6, :]
        u = ALPHA * x_ref[s] + g2 * f[s * TT:(s + 1) * TT]
        y_ref[s] = _ln_rows(u) * lng_ref[...] + lnb_ref[...]


def _ffn_call(x, mod, wup, wdn, lng, lnb, *, SB, TT):
    nseq, T, _ = x.shape
    TM = SB * TT
    kernel = functools.partial(_ffn_kernel, SB=SB, TT=TT)
    return pl.pallas_call(
        kernel,
        grid=(nseq // SB, T // TT),
        in_specs=[
            pl.BlockSpec((SB, TT, D_MODEL), lambda i, t: (i, t, 0)),
            pl.BlockSpec((SB, 6, D_MODEL), lambda i, t: (i, 0, 0)),
            _const_spec((D_MODEL, 2 * D_FF)),
            _const_spec((D_FF, D_MODEL)),
            _const_spec((1, D_MODEL)),
            _const_spec((1, D_MODEL)),
        ],
        out_specs=pl.BlockSpec((SB, TT, D_MODEL), lambda i, t: (i, t, 0)),
        out_shape=jax.ShapeDtypeStruct((nseq, T, D_MODEL), F32),
        scratch_shapes=[
            pltpu.VMEM((TM, D_MODEL), BF16),
            pltpu.VMEM((TM, D_FF), BF16),
        ],
        compiler_params=pltpu.CompilerParams(
            dimension_semantics=("arbitrary", "arbitrary"), vmem_limit_bytes=VMEM_LIMIT_BYTES),
        name="ffn",
    )(x, mod, wup, wdn, lng, lnb)


def kernel(x_prompt, x_sample, cache_attn_k, cache_attn_v, state_hgrn, c_prompt, c_sample, w_ada, b_ada, w_in, attn_sinks, hgrn_lb_logits, hgrn_norm_w, w_branch_attn, w_branch_hgrn, w_out, ln_mix_g, ln_mix_b, w_up, w_down, ln_ffn_g, ln_ffn_b):
    assert w_ada.shape[0] == DEPTH and hgrn_lb_logits.shape[0] == DEPTH + 1
    nb = x_prompt.shape[0]
    ns = x_sample.shape[0]
    assert nb + ns <= ADA_ROWS and x_sample.shape[1] == CHUNK

    c_all = jnp.concatenate([c_prompt, c_sample, jnp.zeros((ADA_ROWS - nb - ns, D_MODEL), F32)], axis=0)
    mod = _ada_call(c_all, w_ada[0], b_ada[0][None, :]).reshape(ADA_ROWS, 6, D_MODEL)
    mod_p = mod[:nb]
    mod_s = mod[nb:nb + ns]

    mixer_w = (w_in[0].astype(BF16), w_branch_attn[0].astype(BF16), w_branch_hgrn[0].astype(BF16),
               w_out[0].astype(BF16), attn_sinks[0], hgrn_lb_logits, hgrn_norm_w[0][None, :],
               ln_mix_g[0][None, :], ln_mix_b[0][None, :])
    ffn_w = (w_up[0].astype(BF16), w_down[0].astype(BF16), ln_ffn_g[0][None, :], ln_ffn_b[0][None, :])

    zk = jnp.zeros((nb, WINDOW, KV_WIDTH), F32)
    zs = jnp.zeros((nb, HGRN_HEADS, HGRN_DK, HGRN_DV), F32)
    x1p, kp, vp, sp = _mixer_call(x_prompt, mod_p, zk, zk, zs, *mixer_w, SB=1, TT=PROMPT_TILE, pos0=0)
    x1s, ks, vs, ss = _mixer_call(
        x_sample, mod_s,
        cache_attn_k[0].reshape(ns, WINDOW, KV_WIDTH), cache_attn_v[0].reshape(ns, WINDOW, KV_WIDTH),
        state_hgrn[0], *mixer_w, SB=MIXER_SAMPLE_SEQS, TT=CHUNK, pos0=PAST_LEN)

    yp = _ffn_call(x1p, mod_p, *ffn_w, SB=1, TT=PROMPT_TILE)
    ys = _ffn_call(x1s, mod_s, *ffn_w, SB=FFN_SAMPLE_SEQS, TT=CHUNK)

    win = lambda a: a.reshape(1, a.shape[0], WINDOW, N_KV_HEADS, HEAD_DIM)
    return (yp, ys, win(kp), win(vp), sp[None], win(ks), win(vs), ss[None])
```

```python
import functools

import jax
import jax.numpy as jnp
from jax import lax
from jax.experimental import pallas as pl
from jax.experimental.pallas import tpu as pltpu

D_MODEL = 1024
PAST_LEN = 1024
CHUNK = 64
WINDOW = 128
HEAD_DIM = 64
N_Q_HEADS = 8
N_KV_HEADS = 2
Q_GROUP = N_Q_HEADS // N_KV_HEADS
ATTN_WIDTH = N_Q_HEADS * HEAD_DIM
KV_WIDTH = N_KV_HEADS * HEAD_DIM
GROUP_WIDTH = Q_GROUP * HEAD_DIM
HGRN_HEADS = 8
HGRN_DK = 128
HGRN_DV = 128
HGRN_WIDTH = HGRN_HEADS * HGRN_DK
D_FF = 2816
IN_WIDTH = ATTN_WIDTH + 2 * KV_WIDTH + 4 * HGRN_WIDTH + 2 * D_MODEL
DEPTH = 1
ALPHA = (2 * DEPTH) ** 0.25
LN_EPS = 1e-5
RMS_EPS = 1e-6

Q0 = 0
K0 = Q0 + ATTN_WIDTH
V0 = K0 + KV_WIDTH
HQ0 = V0 + KV_WIDTH
HF0 = HQ0 + HGRN_WIDTH
HI0 = HF0 + HGRN_WIDTH
HG0 = HI0 + HGRN_WIDTH
GA0 = HG0 + HGRN_WIDTH
GH0 = GA0 + D_MODEL

LANES = 128
MXU_WIDTH = 256
VMEM_LIMIT_BYTES = 60 * 1024 * 1024

PROMPT_TILE = 512
FFN_PROMPT_TILE = 1024
MIXER_SAMPLE_SEQS = 4
FFN_SAMPLE_SEQS = 8

HGRN_BLOCK = 64
HGRN_HALF = HGRN_BLOCK // 2
KEY_WIN = 256
BAND_PAD = KEY_WIN - WINDOW - CHUNK
ADA_ROWS = 40
ADA_BLOCK = 1024

BF16 = jnp.bfloat16
F32 = jnp.float32


def _sigmoid(x):
    return 1.0 / (1.0 + jnp.exp(-x))


def _ln_rows(x):
    mu = jnp.mean(x, axis=-1, keepdims=True)
    xc = x - mu
    var = jnp.mean(xc * xc, axis=-1, keepdims=True)
    return xc * lax.rsqrt(var + LN_EPS)


def _dot(a, b):
    return jnp.dot(a, b, preferred_element_type=F32)


def _dot_nt(a, b):
    return lax.dot_general(a, b, (((1,), (1,)), ((), ())), preferred_element_type=F32)


def _cumsum_rows(x):
    n = x.shape[0]
    row = lax.broadcasted_iota(jnp.int32, x.shape, 0)
    s = 1
    while s < n:
        x = x + jnp.where(row >= s, pltpu.roll(x, s, axis=0), 0.0)
        s *= 2
    return x


def _ada_kernel(c_ref, w_ref, b_ref, o_ref):
    c = c_ref[...]
    a = c * _sigmoid(c)
    o_ref[...] = _dot(a.astype(BF16), w_ref[...].astype(BF16)) + b_ref[...]


def _ada_call(c_all, w_ada, b_ada):
    n = w_ada.shape[1]
    return pl.pallas_call(
        _ada_kernel,
        grid=(n // ADA_BLOCK,),
        in_specs=[
            pl.BlockSpec((ADA_ROWS, D_MODEL), lambda j: (0, 0)),
            pl.BlockSpec((D_MODEL, ADA_BLOCK), lambda j: (0, j)),
            pl.BlockSpec((1, ADA_BLOCK), lambda j: (0, j)),
        ],
        out_specs=pl.BlockSpec((ADA_ROWS, ADA_BLOCK), lambda j: (0, j)),
        out_shape=jax.ShapeDtypeStruct((ADA_ROWS, n), F32),
        compiler_params=pltpu.CompilerParams(dimension_semantics=("arbitrary",)),
        name="ada_mod",
    )(c_all, w_ada, b_ada)


def _mixer_kernel(x_ref, mod_ref, ck_ref, cv_ref, st_ref, w_in_ref, wba_ref, wbh_ref, wout_ref,
                  sinks_ref, lbl_ref, nw_ref, lng_ref, lnb_ref,
                  y_ref, kw_ref, vw_ref, sn_ref,
                  h_sc, pq_sc, ph_sc, pi_sc, pg_sc, pz_sc, kb_sc, vb_sc, s_sc, ya_sc, yh_sc,
                  a_sc, s16_sc, qi_sc, v_sc, m16_sc, *, SB, TT, pos0):
    t = pl.program_id(1)
    last_t = pl.num_programs(1) - 1
    n_chunks = TT // CHUNK
    band_rows = WINDOW + TT
    TM = SB * TT
    blocks_per_seq = TT // HGRN_BLOCK
    n_blocks = SB * blocks_per_seq

    @pl.when(t == 0)
    def _():
        for s in range(SB):
            kb_sc[s, 0:WINDOW, :] = ck_ref[s]
            vb_sc[s, 0:WINDOW, :] = cv_ref[s]
            for hh in range(HGRN_HEADS):
                s_sc[s, hh] = st_ref[s, hh]

    for s in range(SB):
        sh1 = mod_ref[s, 0:1, :]
        sc1 = mod_ref[s, 1:2, :]
        h = _ln_rows(x_ref[s]) * (1.0 + sc1) + sh1
        h_sc[s * TT:(s + 1) * TT, :] = h.astype(BF16)

    def project(dst, w_col0, j):
        cols = slice(j * MXU_WIDTH, (j + 1) * MXU_WIDTH)
        wcols = slice(w_col0 + j * MXU_WIDTH, w_col0 + (j + 1) * MXU_WIDTH)
        dst[:, cols] = _dot(h_sc[...], w_in_ref[:, wcols]).astype(dst.dtype)

    proj_q = [functools.partial(project, pq_sc, Q0, j) for j in range((HQ0 - Q0) // MXU_WIDTH)]
    proj_h = ([functools.partial(project, ph_sc, HQ0, j) for j in range((HI0 - HQ0) // MXU_WIDTH)]
              + [functools.partial(project, pi_sc, HI0, j) for j in range((HG0 - HI0) // MXU_WIDTH)])
    proj_g = [functools.partial(project, pg_sc, HG0, j) for j in range((GA0 - HG0) // MXU_WIDTH)]
    proj_z = [functools.partial(project, pz_sc, GA0, j) for j in range((IN_WIDTH - GA0) // MXU_WIDTH)]

    l0 = lbl_ref[0:1, :]
    l1 = lbl_ref[1:2, :]
    lmax = jnp.maximum(l0, l1)
    e0 = jnp.exp(l0 - lmax)
    lb = e0 / (e0 + jnp.exp(l1 - lmax))
    oml = 1.0 - lb
    in_h1 = lax.broadcasted_iota(jnp.int32, (HGRN_BLOCK, 1), 0) < HGRN_HALF
    causal = (lax.broadcasted_iota(jnp.int32, (HGRN_BLOCK, HGRN_BLOCK), 0)
              >= lax.broadcasted_iota(jnp.int32, (HGRN_BLOCK, HGRN_BLOCK), 1))
    m1 = HGRN_HALF // 2 - 1
    m2 = HGRN_HALF + m1

    def prep_block(c):
        rows = slice(c * HGRN_BLOCK, (c + 1) * HGRN_BLOCK)
        sg = _sigmoid(ph_sc[rows, HGRN_WIDTH:2 * HGRN_WIDTH])
        log_f = jnp.log(lb + oml * sg)
        k_in = oml * (1.0 - sg)
        b = _cumsum_rows(log_f)
        r1 = b[m1:m1 + 1, :]
        r2 = b[m2:m2 + 1, :]
        b_last = b[HGRN_BLOCK - 1:HGRN_BLOCK, :]
        rh = jnp.where(in_h1, r1, r2)
        hq = ph_sc[rows, 0:HGRN_WIDTH]
        qe = hq * _sigmoid(hq) * jnp.exp(b - rh)
        ke = k_in * jnp.exp(rh - b)
        qi = qe * jnp.where(in_h1, jnp.exp(r1), jnp.exp(r2))
        kd = ke * jnp.where(in_h1, jnp.exp(b_last - r1), jnp.exp(b_last - r2))
        la = jnp.where(in_h1, qe, qe * jnp.exp(r2 - r1))
        lb2 = jnp.where(in_h1, 0.0, qe)
        ra = jnp.where(in_h1, ke, 0.0)
        rb = jnp.where(in_h1, 0.0, ke)
        dec_col = jnp.exp(b_last).T
        la16, lb16, ra16, rb16, qi16, kdt16 = (a.astype(BF16) for a in (la, lb2, ra, rb, qi, kd.T))
        v16 = pi_sc[rows, :]
        s = c // blocks_per_seq
        for hh in range(HGRN_HEADS):
            sl = slice(hh * HGRN_DK, (hh + 1) * HGRN_DK)
            lhs = jnp.concatenate([la16[:, sl], lb16[:, sl]], axis=1)
            rhs = jnp.concatenate([ra16[:, sl], rb16[:, sl]], axis=1)
            a_sc[hh, rows, :] = jnp.where(causal, _dot_nt(lhs, rhs), 0.0).astype(BF16)
            qi_sc[hh, rows, :] = qi16[:, sl]
            v_sc[hh, rows, :] = v16[:, sl]
            st = s_sc[s, hh]
            s16_sc[hh, c] = st.astype(BF16)
            s_sc[s, hh] = dec_col[sl, :] * st + _dot(kdt16[sl, :], v16[:, sl])

    lane = lax.broadcasted_iota(jnp.int32, (1, LANES), 1)
    row = lax.broadcasted_iota(jnp.int32, (Q_GROUP * CHUNK, 1), 0)
    head_of_row = row >> 6
    q_of_row = row & (CHUNK - 1)
    key = lax.broadcasted_iota(jnp.int32, (1, KEY_WIN), 1)
    dist = jnp.abs(q_of_row + WINDOW - key).astype(F32)
    first_half = lane < HEAD_DIM

    @functools.cache
    def attn_operands(s, g):
        kband = kb_sc[s]
        vband = vb_sc[s]
        kroll = pltpu.roll(kband, HEAD_DIM, axis=1)
        vroll = pltpu.roll(vband, HEAD_DIM, axis=1)
        kk = jnp.where(first_half, kband if g == 0 else kroll, 0.0).astype(BF16)
        v1 = jnp.where(first_half, vband, vroll) if g == 0 else jnp.where(first_half, vroll, vband)
        vv = v1.astype(BF16)
        slope = jnp.zeros((Q_GROUP * CHUNK, 1), F32)
        sink = jnp.zeros((Q_GROUP * CHUNK, 1), F32)
        for hq in range(Q_GROUP):
            slope = jnp.where(head_of_row == hq, 2.0 ** -(g * Q_GROUP + hq + 1), slope)
            sink = jnp.where(head_of_row == hq, sinks_ref[g * Q_GROUP + hq], sink)
        return kk, vv, slope * dist, sink

    def attn_chunk(s, g, c):
        kk, vv, bias, sink = attn_operands(s, g)
        rq = s * TT + c * CHUNK
        qg = pq_sc[rq:rq + CHUNK, g * GROUP_WIDTH:(g + 1) * GROUP_WIDTH] * (HEAD_DIM ** -0.5)
        parts = []
        for j in range(GROUP_WIDTH // LANES):
            pair = qg[:, j * LANES:(j + 1) * LANES]
            parts.append(jnp.where(first_half, pair, 0.0))
            parts.append(jnp.where(first_half, pltpu.roll(pair, HEAD_DIM, axis=1), 0.0))
        lhs = jnp.concatenate(parts, axis=0).astype(BF16)
        kwin = kk[c * CHUNK:c * CHUNK + KEY_WIN]
        vwin = vv[c * CHUNK:c * CHUNK + KEY_WIN]
        sc = _dot_nt(lhs, kwin) - bias
        ok = key < WINDOW + CHUNK
        if pos0 < WINDOW:
            ok = ok & (key >= WINDOW - pos0 - c * CHUNK - t * TT)
        sc = jnp.where(ok, sc, -jnp.inf)
        m = jnp.maximum(jnp.max(sc, axis=-1, keepdims=True), sink)
        e = jnp.exp(sc - m)
        den = jnp.sum(e, axis=-1, keepdims=True) + jnp.exp(sink - m)
        r = _dot(e.astype(BF16), vwin) / den
        out = jnp.concatenate(
            [jnp.where(first_half, r[2 * j * CHUNK:(2 * j + 1) * CHUNK], r[(2 * j + 1) * CHUNK:(2 * j + 2) * CHUNK])
             for j in range(GROUP_WIDTH // LANES)], axis=1)
        ya_sc[rq:rq + CHUNK, g * GROUP_WIDTH:(g + 1) * GROUP_WIDTH] = out.astype(BF16)

    def fill_bands():
        for s in range(SB):
            r0 = s * TT
            kb_sc[s, WINDOW:band_rows, :] = pq_sc[r0:r0 + TT, K0:K0 + KV_WIDTH]
            vb_sc[s, WINDOW:band_rows, :] = pq_sc[r0:r0 + TT, V0:V0 + KV_WIDTH]
            kb_sc[s, band_rows:, :] = jnp.zeros((BAND_PAD, KV_WIDTH), F32)
            vb_sc[s, band_rows:, :] = jnp.zeros((BAND_PAD, KV_WIDTH), F32)
            kw_ref[s] = kb_sc[s, TT:TT + WINDOW, :]
            vw_ref[s] = vb_sc[s, TT:TT + WINDOW, :]

    nw = nw_ref[...]

    def hgrn_outputs(hh):
        sl = slice(hh * HGRN_DV, (hh + 1) * HGRN_DV)
        for c in range(n_blocks):
            rows = slice(c * HGRN_BLOCK, (c + 1) * HGRN_BLOCK)
            o = _dot(a_sc[hh, rows, :], v_sc[hh, rows, :]) + _dot(qi_sc[hh, rows, :], s16_sc[hh, c])
            o = o * lax.rsqrt(jnp.mean(o * o, axis=-1, keepdims=True) + RMS_EPS)
            hg = pg_sc[rows, sl]
            yh_sc[rows, sl] = (o * nw[:, sl] * (hg * _sigmoid(hg))).astype(BF16)

    def interleave(units, fillers):
        for i, unit in enumerate(units):
            unit()
            for p in fillers[i * len(fillers) // len(units):(i + 1) * len(fillers) // len(units)]:
                p()

    def attn_group(s, g):
        for c in range(n_chunks):
            attn_chunk(s, g, c)

    for p in proj_h:
        p()
    interleave([functools.partial(prep_block, c) for c in range(n_blocks)], proj_q + proj_g)
    fill_bands()
    interleave([functools.partial(attn_group, s, g) for s in range(SB) for g in range(N_KV_HEADS)], proj_z)
    for hh in range(HGRN_HEADS):
        hgrn_outputs(hh)

    for s in range(SB):
        knext = kb_sc[s, TT:TT + WINDOW, :]
        vnext = vb_sc[s, TT:TT + WINDOW, :]
        kb_sc[s, 0:WINDOW, :] = knext
        vb_sc[s, 0:WINDOW, :] = vnext

    for j in range(D_MODEL // MXU_WIDTH):
        cols = slice(j * MXU_WIDTH, (j + 1) * MXU_WIDTH)
        gcols = slice(D_MODEL + j * MXU_WIDTH, D_MODEL + (j + 1) * MXU_WIDTH)
        pa = _dot(ya_sc[...], wba_ref[:, cols])
        ph = _dot(yh_sc[...], wbh_ref[:, cols])
        merged = _sigmoid(pz_sc[:, cols]) * pa + _sigmoid(pz_sc[:, gcols]) * ph
        m16_sc[:, cols] = merged.astype(BF16)
    mo = _dot(m16_sc[...], wout_ref[...])
    for s in range(SB):
        g1 = mod_ref[s, 2:3, :]
        u = ALPHA * x_ref[s] + g1 * mo[s * TT:(s + 1) * TT]
        y_ref[s] = _ln_rows(u) * lng_ref[...] + lnb_ref[...]

    @pl.when(t == last_t)
    def _():
        for s in range(SB):
            for hh in range(HGRN_HEADS):
                sn_ref[s, hh] = s_sc[s, hh]


def _const_spec(shape):
    zeros = (0,) * len(shape)
    return pl.BlockSpec(shape, lambda i, t: zeros, pipeline_mode=pl.Buffered(1))


def _mixer_call(x, mod, ck, cv, st, w_in, wba, wbh, wout, sinks, lbl, nw, lng, lnb, *, SB, TT, pos0):
    nseq, T, _ = x.shape
    TM = SB * TT
    grid = (nseq // SB, T // TT)
    seq_map3 = lambda i, t: (i, 0, 0)
    seq_map4 = lambda i, t: (i, 0, 0, 0)
    kernel = functools.partial(_mixer_kernel, SB=SB, TT=TT, pos0=pos0)
    return pl.pallas_call(
        kernel,
        grid=grid,
        in_specs=[
            pl.BlockSpec((SB, TT, D_MODEL), lambda i, t: (i, t, 0)),
            pl.BlockSpec((SB, 6, D_MODEL), seq_map3),
            pl.BlockSpec((SB, WINDOW, KV_WIDTH), seq_map3),
            pl.BlockSpec((SB, WINDOW, KV_WIDTH), seq_map3),
            pl.BlockSpec((SB, HGRN_HEADS, HGRN_DK, HGRN_DV), seq_map4),
            _const_spec((D_MODEL, IN_WIDTH)),
            _const_spec((ATTN_WIDTH, D_MODEL)),
            _const_spec((HGRN_WIDTH, D_MODEL)),
            _const_spec((D_MODEL, D_MODEL)),
            pl.BlockSpec(memory_space=pltpu.SMEM),
            _const_spec((2, HGRN_WIDTH)),
            _const_spec((1, HGRN_WIDTH)),
            _const_spec((1, D_MODEL)),
            _const_spec((1, D_MODEL)),
        ],
        out_specs=[
            pl.BlockSpec((SB, TT, D_MODEL), lambda i, t: (i, t, 0)),
            pl.BlockSpec((SB, WINDOW, KV_WIDTH), seq_map3),
            pl.BlockSpec((SB, WINDOW, KV_WIDTH), seq_map3),
            pl.BlockSpec((SB, HGRN_HEADS, HGRN_DK, HGRN_DV), seq_map4),
        ],
        out_shape=[
            jax.ShapeDtypeStruct((nseq, T, D_MODEL), F32),
            jax.ShapeDtypeStruct((nseq, WINDOW, KV_WIDTH), F32),
            jax.ShapeDtypeStruct((nseq, WINDOW, KV_WIDTH), F32),
            jax.ShapeDtypeStruct((nseq, HGRN_HEADS, HGRN_DK, HGRN_DV), F32),
        ],
        scratch_shapes=[
            pltpu.VMEM((TM, D_MODEL), BF16),
            pltpu.VMEM((TM, HQ0 - Q0), F32),
            pltpu.VMEM((TM, HI0 - HQ0), F32),
            pltpu.VMEM((TM, HG0 - HI0), BF16),
            pltpu.VMEM((TM, GA0 - HG0), F32),
            pltpu.VMEM((TM, IN_WIDTH - GA0), F32),
            pltpu.VMEM((SB, WINDOW + TT + BAND_PAD, KV_WIDTH), F32),
            pltpu.VMEM((SB, WINDOW + TT + BAND_PAD, KV_WIDTH), F32),
            pltpu.VMEM((SB, HGRN_HEADS, HGRN_DK, HGRN_DV), F32),
            pltpu.VMEM((TM, ATTN_WIDTH), BF16),
            pltpu.VMEM((TM, HGRN_WIDTH), BF16),
            pltpu.VMEM((HGRN_HEADS, TM, HGRN_BLOCK), BF16),
            pltpu.VMEM((HGRN_HEADS, TM // HGRN_BLOCK, HGRN_DK, HGRN_DV), BF16),
            pltpu.VMEM((HGRN_HEADS, TM, HGRN_DK), BF16),
            pltpu.VMEM((HGRN_HEADS, TM, HGRN_DV), BF16),
            pltpu.VMEM((TM, D_MODEL), BF16),
        ],
        compiler_params=pltpu.CompilerParams(
            dimension_semantics=("arbitrary", "arbitrary"), vmem_limit_bytes=VMEM_LIMIT_BYTES),
        name="mixer",
    )(x, mod, ck, cv, st, w_in, wba, wbh, wout, sinks, lbl, nw, lng, lnb)


def _ffn_kernel(x_ref, mod_ref, wup_ref, wdn_ref, lng_ref, lnb_ref, y_ref, h_sc, act_sc, *, SB, TT):
    for s in range(SB):
        sh2 = mod_ref[s, 3:4, :]
        sc2 = mod_ref[s, 4:5, :]
        h = _ln_rows(x_ref[s]) * (1.0 + sc2) + sh2
        h_sc[s * TT:(s + 1) * TT, :] = h.astype(BF16)
    for j in range(D_FF // MXU_WIDTH):
        cols = slice(j * MXU_WIDTH, (j + 1) * MXU_WIDTH)
        gcols = slice(D_FF + j * MXU_WIDTH, D_FF + (j + 1) * MXU_WIDTH)
        u = _dot(h_sc[...], wup_ref[:, cols])
        gt = _dot(h_sc[...], wup_ref[:, gcols])
        act_sc[:, cols] = (gt * _sigmoid(gt) * u).astype(BF16)
    f = _dot(act_sc[...], wdn_ref[...])
    for s in range(SB):
        g2 = mod_ref[s, 5:6, :]
        u = ALPHA * x_ref[s] + g2 * f[s * TT:(s + 1) * TT]
        y_ref[s] = _ln_rows(u) * lng_ref[...] + lnb_ref[...]


def _ffn_call(x, mod, wup, wdn, lng, lnb, *, SB, TT):
    nseq, T, _ = x.shape
    TM = SB * TT
    kernel = functools.partial(_ffn_kernel, SB=SB, TT=TT)
    return pl.pallas_call(
        kernel,
        grid=(nseq // SB, T // TT),
        in_specs=[
            pl.BlockSpec((SB, TT, D_MODEL), lambda i, t: (i, t, 0)),
            pl.BlockSpec((SB, 6, D_MODEL), lambda i, t: (i, 0, 0)),
            _const_spec((D_MODEL, 2 * D_FF)),
            _const_spec((D_FF, D_MODEL)),
            _const_spec((1, D_MODEL)),
            _const_spec((1, D_MODEL)),
        ],
        out_specs=pl.BlockSpec((SB, TT, D_MODEL), lambda i, t: (i, t, 0)),
        out_shape=jax.ShapeDtypeStruct((nseq, T, D_MODEL), F32),
        scratch_shapes=[
            pltpu.VMEM((TM, D_MODEL), BF16),
            pltpu.VMEM((TM, D_FF), BF16),
        ],
        compiler_params=pltpu.CompilerParams(
            dimension_semantics=("arbitrary", "arbitrary"), vmem_limit_bytes=VMEM_LIMIT_BYTES),
        name="ffn",
    )(x, mod, wup, wdn, lng, lnb)


def kernel(x_prompt, x_sample, cache_attn_k, cache_attn_v, state_hgrn, c_prompt, c_sample, w_ada, b_ada, w_in, attn_sinks, hgrn_lb_logits, hgrn_norm_w, w_branch_attn, w_branch_hgrn, w_out, ln_mix_g, ln_mix_b, w_up, w_down, ln_ffn_g, ln_ffn_b):
    assert w_ada.shape[0] == DEPTH and hgrn_lb_logits.shape[0] == DEPTH + 1
    nb = x_prompt.shape[0]
    ns = x_sample.shape[0]
    assert nb + ns <= ADA_ROWS and x_sample.shape[1] == CHUNK

    c_all = jnp.concatenate([c_prompt, c_sample, jnp.zeros((ADA_ROWS - nb - ns, D_MODEL), F32)], axis=0)
    mod = _ada_call(c_all, w_ada[0], b_ada[0][None, :]).reshape(ADA_ROWS, 6, D_MODEL)
    mod_p = mod[:nb]
    mod_s = mod[nb:nb + ns]

    mixer_w = (w_in[0].astype(BF16), w_branch_attn[0].astype(BF16), w_branch_hgrn[0].astype(BF16),
               w_out[0].astype(BF16), attn_sinks[0], hgrn_lb_logits, hgrn_norm_w[0][None, :],
               ln_mix_g[0][None, :], ln_mix_b[0][None, :])
    ffn_w = (w_up[0].astype(BF16), w_down[0].astype(BF16), ln_ffn_g[0][None, :], ln_ffn_b[0][None, :])

    zk = jnp.zeros((nb, WINDOW, KV_WIDTH), F32)
    zs = jnp.zeros((nb, HGRN_HEADS, HGRN_DK, HGRN_DV), F32)
    x1p, kp, vp, sp = _mixer_call(x_prompt, mod_p, zk, zk, zs, *mixer_w, SB=1, TT=PROMPT_TILE, pos0=0)
    x1s, ks, vs, ss = _mixer_call(
        x_sample, mod_s,
        cache_attn_k[0].reshape(ns, WINDOW, KV_WIDTH), cache_attn_v[0].reshape(ns, WINDOW, KV_WIDTH),
        state_hgrn[0], *mixer_w, SB=MIXER_SAMPLE_SEQS, TT=CHUNK, pos0=PAST_LEN)

    yp = _ffn_call(x1p, mod_p, *ffn_w, SB=1, TT=FFN_PROMPT_TILE)
    ys = _ffn_call(x1s, mod_s, *ffn_w, SB=FFN_SAMPLE_SEQS, TT=CHUNK)

    win = lambda a: a.reshape(1, a.shape[0], WINDOW, N_KV_HEADS, HEAD_DIM)
    return (yp, ys, win(kp), win(vp), sp[None], win(ks), win(vs), ss[None])
```
